```python
import math
import jax, jax.numpy as jnp
from jax import lax
import numpy as np

D_MODEL = 1024
BATCH = 1
SEQ = 16384
DEPTH = 1
DEC_BATCH = 16
DEC_SEQ = 4096
PAST_LEN = 128

D_MIX = D_MODEL
D_FOURIER = D_MIX // 2
D_SSM = D_MIX - D_FOURIER
N_FOURIER_HEADS = 4
FOURIER_HEAD_DIM = D_FOURIER // N_FOURIER_HEADS
SSM_GROUP = 16
N_SSM_GROUPS = D_SSM // SSM_GROUP
SSM_STATE = 64
N_DIR = 2
D_FF = 4 * D_MODEL
N_MOD = 6
EPS = 1e-6
DT_MIN = 1e-3
DT_MAX = 1e-1
A_RE_MAX = -1e-4

kernel_name = "hymba_fnet_s5_adaln_encoder"


def rmsnorm(x, g):
    xf = x.astype(jnp.float32)
    y = xf * lax.rsqrt(jnp.mean(xf * xf, axis=-1, keepdims=True) + EPS)
    return (y * g.astype(jnp.float32)).astype(x.dtype)


def fourier_mixer(u, w_f, b_f):
    b, l, _ = u.shape
    z = jnp.fft.fft2(u.astype(jnp.float32), axes=(1, 2), norm="ortho").real.astype(u.dtype)
    z = z.reshape(b, l, N_FOURIER_HEADS, FOURIER_HEAD_DIM)
    y = jnp.einsum("blhd,hde->blhe", z, w_f).reshape(b, l, D_FOURIER)
    return y + b_f


def _scan_combine(left, right):
    a_l, b_l = left
    a_r, b_r = right
    return a_r * a_l, a_r * b_l + b_r


def s5_direction(u_c, a_re, a_im, log_dt, b_re, b_im, c_re, c_im, reverse):
    f32 = jnp.float32
    lam = lax.complex(jnp.minimum(a_re.astype(f32), A_RE_MAX), a_im.astype(f32))
    dt = jnp.exp(log_dt.astype(f32))[:, None]
    a_bar = jnp.exp(lam * dt)
    b_mat = lax.complex(b_re.astype(f32), b_im.astype(f32))
    b_bar = ((a_bar - 1.0) / lam)[..., None] * b_mat
    bu = jnp.einsum("blgh,gph->blgp", u_c, b_bar)
    a_seq = jnp.broadcast_to(a_bar, bu.shape)
    _, h = lax.associative_scan(_scan_combine, (a_seq, bu), axis=1, reverse=reverse)
    c_mat = lax.complex(c_re.astype(f32), c_im.astype(f32))
    return jnp.einsum("blgp,ghp->blgh", h, c_mat).real


def s5_mixer(u, a_re, a_im, log_dt, b_re, b_im, c_re, c_im, d_skip, w_glu, b_glu):
    b, l, _ = u.shape
    uf = u.astype(jnp.float32).reshape(b, l, N_SSM_GROUPS, SSM_GROUP)
    u_c = lax.complex(uf, jnp.zeros_like(uf))
    y_fwd = s5_direction(u_c, a_re[0], a_im[0], log_dt[0], b_re[0], b_im[0], c_re[0], c_im[0], False)
    y_bwd = s5_direction(u_c, a_re[1], a_im[1], log_dt[1], b_re[1], b_im[1], c_re[1], c_im[1], True)
    y = (y_fwd + y_bwd).reshape(b, l, D_SSM) + d_skip.astype(jnp.float32) * uf.reshape(b, l, D_SSM)
    g = jax.nn.gelu(y).astype(u.dtype)
    ab = g @ w_glu + b_glu
    val, gate = jnp.split(ab, 2, axis=-1)
    return val * jax.nn.sigmoid(gate)


def encoder_layer(x, c, w_ada, b_ada, g_mix_norm, w_in, w_fourier, b_fourier,
                  ssm_a_re, ssm_a_im, ssm_log_dt, ssm_b_re, ssm_b_im, ssm_c_re, ssm_c_im,
                  ssm_d, w_glu, b_glu, g_fourier_out, g_ssm_out, w_out,
                  g_mlp_norm, w_mlp_in, b_mlp_in, w_mlp_out, b_mlp_out):
    mod = (jax.nn.silu(c) @ w_ada + b_ada)[:, None, :]
    shift1, scale1, gate1, shift2, scale2, gate2 = jnp.split(mod, N_MOD, axis=-1)
    h = rmsnorm(x, g_mix_norm) * (1.0 + scale1) + shift1
    z = h @ w_in
    y_f = fourier_mixer(z[..., :D_FOURIER], w_fourier, b_fourier)
    y_s = s5_mixer(z[..., D_FOURIER:], ssm_a_re, ssm_a_im, ssm_log_dt, ssm_b_re, ssm_b_im,
                   ssm_c_re, ssm_c_im, ssm_d, w_glu, b_glu)
    merged = jnp.concatenate([rmsnorm(y_f, g_fourier_out), rmsnorm(y_s, g_ssm_out)], axis=-1)
    x = x + gate1 * (merged @ w_out)
    h = rmsnorm(x, g_mlp_norm) * (1.0 + scale2) + shift2
    ff = jnp.square(jax.nn.relu(h @ w_mlp_in + b_mlp_in)) @ w_mlp_out + b_mlp_out
    return x + gate2 * ff


def setup_inputs(seed: int = 0) -> dict:
    key = jax.random.key(seed)
    ks = jax.random.split(key, 32)
    f32 = jnp.float32
    nrm = lambda k, shape, s: (jax.random.normal(k, shape, f32) * s).astype(f32)
    G, P, H = N_SSM_GROUPS, SSM_STATE, SSM_GROUP
    a_im_base = jnp.pi * jnp.arange(P, dtype=f32)
    return {
        "x_prompt": nrm(ks[0], (BATCH, SEQ, D_MODEL), 1.0),
        "x_sample": nrm(ks[1], (DEC_BATCH, DEC_SEQ, D_MODEL), 1.0),
        "c_prompt": nrm(ks[2], (BATCH, D_MODEL), 1.0),
        "c_sample": nrm(ks[3], (DEC_BATCH, D_MODEL), 1.0),
        "w_ada": nrm(ks[4], (DEPTH, D_MODEL, N_MOD * D_MODEL), 0.5 * D_MODEL ** -0.5),
        "b_ada": nrm(ks[5], (DEPTH, N_MOD * D_MODEL), 0.02),
        "g_mix_norm": 1.0 + nrm(ks[6], (DEPTH, D_MODEL), 0.02),
        "w_in": nrm(ks[7], (DEPTH, D_MODEL, D_MIX), D_MODEL ** -0.5),
        "w_fourier": nrm(ks[8], (DEPTH, N_FOURIER_HEADS, FOURIER_HEAD_DIM, FOURIER_HEAD_DIM), FOURIER_HEAD_DIM ** -0.5),
        "b_fourier": nrm(ks[9], (DEPTH, D_FOURIER), 0.02),
        "ssm_a_re": -0.5 + nrm(ks[10], (DEPTH, N_DIR, G, P), 0.01),
        "ssm_a_im": a_im_base + nrm(ks[11], (DEPTH, N_DIR, G, P), 0.01),
        "ssm_log_dt": jax.random.uniform(ks[12], (DEPTH, N_DIR, G), f32, math.log(DT_MIN), math.log(DT_MAX)),
        "ssm_b_re": nrm(ks[13], (DEPTH, N_DIR, G, P, H), (2 * H) ** -0.5),
        "ssm_b_im": nrm(ks[14], (DEPTH, N_DIR, G, P, H), (2 * H) ** -0.5),
        "ssm_c_re": nrm(ks[15], (DEPTH, N_DIR, G, H, P), P ** -0.5),
        "ssm_c_im": nrm(ks[16], (DEPTH, N_DIR, G, H, P), P ** -0.5),
        "ssm_d": nrm(ks[17], (DEPTH, D_SSM), 0.5),
        "w_glu": nrm(ks[18], (DEPTH, D_SSM, 2 * D_SSM), D_SSM ** -0.5),
        "b_glu": nrm(ks[19], (DEPTH, 2 * D_SSM), 0.02),
        "g_fourier_out": 1.0 + nrm(ks[20], (DEPTH, D_FOURIER), 0.02),
        "g_ssm_out": 1.0 + nrm(ks[21], (DEPTH, D_SSM), 0.02),
        "w_out": nrm(ks[22], (DEPTH, D_MIX, D_MODEL), D_MIX ** -0.5),
        "g_mlp_norm": 1.0 + nrm(ks[23], (DEPTH, D_MODEL), 0.02),
        "w_mlp_in": nrm(ks[24], (DEPTH, D_MODEL, D_FF), D_MODEL ** -0.5),
        "b_mlp_in": nrm(ks[25], (DEPTH, D_FF), 0.02),
        "w_mlp_out": nrm(ks[26], (DEPTH, D_FF, D_MODEL), D_FF ** -0.5),
        "b_mlp_out": nrm(ks[27], (DEPTH, D_MODEL), 0.02),
        "g_final": 1.0 + nrm(ks[28], (D_MODEL,), 0.02),
    }


def reference(x_prompt, x_sample, c_prompt, c_sample, w_ada, b_ada, g_mix_norm, w_in,
              w_fourier, b_fourier, ssm_a_re, ssm_a_im, ssm_log_dt, ssm_b_re, ssm_b_im,
              ssm_c_re, ssm_c_im, ssm_d, w_glu, b_glu, g_fourier_out, g_ssm_out, w_out,
              g_mlp_norm, w_mlp_in, b_mlp_in, w_mlp_out, b_mlp_out, g_final):
    def trunk(x, c):
        for i in range(DEPTH):
            x = encoder_layer(x, c, w_ada[i], b_ada[i], g_mix_norm[i], w_in[i], w_fourier[i],
                              b_fourier[i], ssm_a_re[i], ssm_a_im[i], ssm_log_dt[i], ssm_b_re[i],
                              ssm_b_im[i], ssm_c_re[i], ssm_c_im[i], ssm_d[i], w_glu[i], b_glu[i],
                              g_fourier_out[i], g_ssm_out[i], w_out[i], g_mlp_norm[i],
                              w_mlp_in[i], b_mlp_in[i], w_mlp_out[i], b_mlp_out[i])
        return rmsnorm(x, g_final)

    y_prompt = trunk(x_prompt, c_prompt)
    y_sample = trunk(x_sample, c_sample)
    return (y_prompt, y_sample)
```

```python
import functools
import math

import numpy as np
import jax
import jax.numpy as jnp
from jax import lax
from jax.experimental import pallas as pl
from jax.experimental.pallas import tpu as pltpu

D_MODEL = 1024
D_FOURIER = 512
D_SSM = 512
N_HEADS = 4
HEAD_DIM = 128
SSM_GROUP = 16
N_GROUPS = 32
SSM_STATE = 64
D_FF = 4096
N_MOD = 6
EPS = 1e-6
A_RE_MAX = -1e-4

SSM_CHUNK = 64
CHUNK_LANES = SSM_CHUNK * SSM_GROUP
STATE_LANES = 2 * SSM_STATE
N_SCAN_LEVELS = 8
DFT_N1 = 128
TOKEN_TILE = 512
FF_CHUNK = 1024
VMEM_LIMIT = 56 * 1024 * 1024

_BF16 = jnp.bfloat16
_F32 = jnp.float32


def _rms(v):
    return v * lax.rsqrt(jnp.mean(v * v, axis=-1, keepdims=True) + EPS)


def _const_spec(shape):
    nd = len(shape)
    return pl.BlockSpec(shape, lambda *_: (0,) * nd)


def _adaln_kernel(c_ref, w_ref, b_ref, o_ref):
    c = c_ref[...]
    s = c * jax.nn.sigmoid(c)
    o_ref[...] = jnp.dot(s, w_ref[...], preferred_element_type=_F32,
                         precision=lax.Precision.HIGHEST) + b_ref[...]


def _adaln(c_all, w_ada, b_ada):
    rows = c_all.shape[0]
    return pl.pallas_call(
        _adaln_kernel,
        grid=(N_MOD,),
        in_specs=[
            _const_spec((rows, D_MODEL)),
            pl.BlockSpec((D_MODEL, D_MODEL), lambda j: (0, j)),
            pl.BlockSpec((1, D_MODEL), lambda j: (0, j)),
        ],
        out_specs=pl.BlockSpec((rows, D_MODEL), lambda j: (0, j)),
        out_shape=jax.ShapeDtypeStruct((rows, N_MOD * D_MODEL), _F32),
        name="adaln",
    )(c_all, w_ada, b_ada.reshape(1, -1))


def _fold_kernel(cs_ref, wf_ref, o_ref):
    o_ref[...] = jnp.dot(cs_ref[...], wf_ref[...], preferred_element_type=_F32,
                         precision=lax.Precision.HIGHEST).astype(o_ref.dtype)


@functools.lru_cache(maxsize=None)
def _channel_dft_np():
    k = np.arange(D_FOURIER)
    ang = 2.0 * np.pi * ((k[:, None] * k[None, :]) % D_FOURIER) / D_FOURIER
    scale = 1.0 / math.sqrt(D_FOURIER)
    return np.concatenate([np.cos(ang), np.sin(ang)], axis=0).astype(np.float32) * scale


def _fold_fourier_weight(w_fourier):
    wblk = jnp.zeros((D_FOURIER, D_FOURIER), _F32)
    for h in range(N_HEADS):
        sl = slice(h * HEAD_DIM, (h + 1) * HEAD_DIM)
        wblk = wblk.at[sl, sl].set(w_fourier[h])
    cs = jnp.asarray(_channel_dft_np())
    return pl.pallas_call(
        _fold_kernel,
        out_shape=jax.ShapeDtypeStruct((2 * D_FOURIER, D_FOURIER), _BF16),
        name="fold_fourier_weight",
    )(cs, wblk)


def _inproj_kernel(x_ref, mod_ref, g_ref, w_ref, zf_ref, zs_ref):
    x = x_ref[...]
    shift = mod_ref[0:1, :]
    scale = mod_ref[1:2, :]
    h = _rms(x) * g_ref[...] * (1.0 + scale) + shift
    z = jnp.dot(h.astype(_BF16), w_ref[...], preferred_element_type=_F32)
    zf_ref[...] = z[:, :D_FOURIER].astype(_BF16)
    zs_ref[...] = z[:, D_FOURIER:].astype(_BF16)


def _inproj(x, mod, g_mix, w_in_bf16):
    b, l, _ = x.shape
    tm = TOKEN_TILE
    return pl.pallas_call(
        _inproj_kernel,
        grid=(b, l // tm),
        in_specs=[
            pl.BlockSpec((None, tm, D_MODEL), lambda i, j: (i, j, 0)),
            pl.BlockSpec((None, N_MOD, D_MODEL), lambda i, j: (i, 0, 0)),
            _const_spec((1, D_MODEL)),
            _const_spec((D_MODEL, D_MODEL)),
        ],
        out_specs=[
            pl.BlockSpec((None, tm, D_FOURIER), lambda i, j: (i, j, 0)),
            pl.BlockSpec((None, tm, D_SSM), lambda i, j: (i, j, 0)),
        ],
        out_shape=[
            jax.ShapeDtypeStruct((b, l, D_FOURIER), _BF16),
            jax.ShapeDtypeStruct((b, l, D_SSM), _BF16),
        ],
        name="inproj",
    )(x, mod, g_mix.reshape(1, -1), w_in_bf16)


@functools.lru_cache(maxsize=None)
def _dft_tables_np(n):
    n1 = DFT_N1
    n2 = n // n1
    i1 = np.arange(n1)
    ang1 = 2.0 * np.pi * ((i1[:, None] * i1[None, :]) % n1) / n1
    f1 = np.concatenate([np.cos(ang1), -np.sin(ang1)], axis=0).astype(np.float32)
    i2 = np.arange(n2)
    m = (i2[None, None, :] * (i1[:, None, None] + n1 * i2[None, :, None])) % n
    ang = 2.0 * np.pi * m / n
    gr = np.cos(ang) / math.sqrt(n)
    gi = -np.sin(ang) / math.sqrt(n)
    g = np.concatenate([np.concatenate([gr, -gi], axis=2),
                        np.concatenate([gi, gr], axis=2)], axis=1).astype(np.float32)
    return f1, g


def _dft1_kernel(f_ref, x_ref, a_ref):
    a_ref[...] = jnp.dot(f_ref[...], x_ref[...],
                         preferred_element_type=_F32).astype(_BF16)


def _dft_stage1(zf, f1):
    b, l, _ = zf.shape
    n1 = DFT_N1
    cols = (l // n1) * D_FOURIER
    cw = 4096
    x2d = zf.reshape(b, n1, cols)
    return pl.pallas_call(
        _dft1_kernel,
        grid=(b, cols // cw),
        in_specs=[
            _const_spec((2 * n1, n1)),
            pl.BlockSpec((None, n1, cw), lambda i, j: (i, 0, j)),
        ],
        out_specs=pl.BlockSpec((None, 2 * n1, cw), lambda i, j: (i, 0, j)),
        out_shape=jax.ShapeDtypeStruct((b, 2 * n1, cols), _BF16),
        name="dft_stage1",
    )(f1, x2d)


def _dft2_kernel(a_ref, g_ref, wc_ref, bf_ref, o_ref, *, kb, n2):
    xr, xi = [], []
    for j in range(kb):
        aa = jnp.concatenate([a_ref[0, j], a_ref[1, j]], axis=0)
        x = jnp.dot(g_ref[j], aa, preferred_element_type=_F32)
        xr.append(x[:n2].astype(_BF16))
        xi.append(x[n2:].astype(_BF16))
    xr = jnp.concatenate(xr, axis=0)
    xi = jnp.concatenate(xi, axis=0)
    y = jnp.dot(xr, wc_ref[:D_FOURIER, :], preferred_element_type=_F32)
    y = y + jnp.dot(xi, wc_ref[D_FOURIER:, :], preferred_element_type=_F32)
    y = y + bf_ref[...]
    o_ref[...] = y.reshape(kb, n2, D_FOURIER).astype(o_ref.dtype)


def _dft_stage2(a, g, wcomb, b_fourier):
    b = a.shape[0]
    n1 = DFT_N1
    n2 = g.shape[1] // 2
    kb = max(8, 1024 // n2)
    a5 = a.reshape(b, 2, n1, n2, D_FOURIER)
    return pl.pallas_call(
        functools.partial(_dft2_kernel, kb=kb, n2=n2),
        grid=(b, n1 // kb),
        in_specs=[
            pl.BlockSpec((None, 2, kb, n2, D_FOURIER), lambda i, j: (i, 0, j, 0, 0)),
            pl.BlockSpec((kb, 2 * n2, 2 * n2), lambda i, j: (j, 0, 0)),
            _const_spec((2 * D_FOURIER, D_FOURIER)),
            _const_spec((1, D_FOURIER)),
        ],
        out_specs=pl.BlockSpec((None, kb, n2, D_FOURIER), lambda i, j: (i, j, 0, 0)),
        out_shape=jax.ShapeDtypeStruct((b, n1, n2, D_FOURIER), _BF16),
        name="dft_stage2",
    )(a5, g, wcomb, b_fourier.reshape(1, -1))


def _fourier_mixer(zf, wcomb, b_fourier):
    b, l, _ = zf.shape
    f1_np, g_np = _dft_tables_np(l)
    f1 = jnp.asarray(f1_np).astype(_BF16)
    g = jnp.asarray(g_np).astype(_BF16)
    a = _dft_stage1(zf, f1)
    y = _dft_stage2(a, g, wcomb, b_fourier)
    return jnp.swapaxes(y, 1, 2).reshape(b, l, D_FOURIER)


def _ssm_operators(a_re, a_im, log_dt, b_re, b_im, c_re, c_im, d_skip):
    t = SSM_CHUNK
    hp = lax.Precision.HIGHEST
    lam_re = jnp.minimum(a_re.astype(_F32), A_RE_MAX)
    lam_im = a_im.astype(_F32)
    dt = jnp.exp(log_dt.astype(_F32))[..., None]
    zr = lam_re * dt
    zi = lam_im * dt
    em1_re = jnp.expm1(zr) * jnp.cos(zi) - 2.0 * jnp.square(jnp.sin(0.5 * zi))
    em1_im = jnp.exp(zr) * jnp.sin(zi)
    den = lam_re * lam_re + lam_im * lam_im
    q_re = (em1_re * lam_re + em1_im * lam_im) / den
    q_im = (em1_im * lam_re - em1_re * lam_im) / den
    bb_re = q_re[..., None] * b_re - q_im[..., None] * b_im
    bb_im = q_re[..., None] * b_im + q_im[..., None] * b_re

    def powers(k):
        kk = k[:, None, None, None]
        mag = jnp.exp(kk * zr[None])
        return mag * jnp.cos(kk * zi[None]), mag * jnp.sin(kk * zi[None])

    pw_re, pw_im = powers(jnp.arange(t + 1, dtype=_F32))
    ca_re = c_re[None] * pw_re[:, :, :, None, :] - c_im[None] * pw_im[:, :, :, None, :]
    ca_im = c_re[None] * pw_im[:, :, :, None, :] + c_im[None] * pw_re[:, :, :, None, :]
    kern = (jnp.einsum("kdghp,dgpi->kdghi", ca_re[:t], bb_re, precision=hp)
            - jnp.einsum("kdghp,dgpi->kdghi", ca_im[:t], bb_im, precision=hp))
    kf, kb = kern[:, 0], kern[:, 1]
    kcat = jnp.concatenate(
        [kb[1:][::-1], (kf[0] + kb[0])[None], kf[1:], jnp.zeros_like(kf[:1])], axis=0)
    lanes = 2 * t * SSM_GROUP
    r2 = jnp.transpose(kcat, (1, 3, 0, 2)).reshape(N_GROUPS, SSM_GROUP, lanes).astype(_BF16)
    skew = jnp.tile(r2, (1, 1, t))[:, :, :t * (lanes - SSM_GROUP)]
    skew = skew.reshape(N_GROUPS, SSM_GROUP, t, lanes - SSM_GROUP)
    off = SSM_GROUP * (t - 1)
    w = skew[:, :, :, off:off + CHUNK_LANES]
    w = jnp.transpose(w, (0, 2, 1, 3)).reshape(N_GROUPS, CHUNK_LANES, CHUNK_LANES)

    def b_side(pre, pim, d):
        re = pre[:, :, None, :] * jnp.swapaxes(bb_re[d], 1, 2)[None] \
            - pim[:, :, None, :] * jnp.swapaxes(bb_im[d], 1, 2)[None]
        im = pre[:, :, None, :] * jnp.swapaxes(bb_im[d], 1, 2)[None] \
            + pim[:, :, None, :] * jnp.swapaxes(bb_re[d], 1, 2)[None]
        return jnp.transpose(re, (1, 0, 2, 3)), jnp.transpose(im, (1, 0, 2, 3))

    f_re, f_im = b_side(pw_re[:t, 0][::-1], pw_im[:t, 0][::-1], 0)
    r_re, r_im = b_side(pw_re[:t, 1], pw_im[:t, 1], 1)
    bm = jnp.concatenate([f_re, f_im, r_re, r_im], axis=-1)
    bm = bm.reshape(N_GROUPS, CHUNK_LANES, 2 * STATE_LANES).astype(_BF16)

    def e_side(cre, cim):
        return jnp.transpose(cre, (1, 3, 0, 2)), -jnp.transpose(cim, (1, 3, 0, 2))

    ef_re, ef_im = e_side(ca_re[1:, 0], ca_im[1:, 0])
    eb_re, eb_im = e_side(ca_re[1:, 1][::-1], ca_im[1:, 1][::-1])
    e = jnp.concatenate([ef_re, ef_im, eb_re, eb_im], axis=1)
    e = e.reshape(N_GROUPS, 2 * STATE_LANES, CHUNK_LANES).astype(_BF16)

    lv = jnp.asarray([float(t * (1 << j)) for j in range(N_SCAN_LEVELS)], _F32)
    lr, li = powers(lv)
    lr = jnp.transpose(lr, (2, 0, 1, 3))
    li = jnp.transpose(li, (2, 0, 1, 3))
    a1 = jnp.concatenate([lr[:, :, 0], lr[:, :, 0], lr[:, :, 1], lr[:, :, 1]], axis=-1)
    a2 = jnp.concatenate([-li[:, :, 0], li[:, :, 0], -li[:, :, 1], li[:, :, 1]], axis=-1)
    coef = jnp.concatenate([a1, a2], axis=1)

    dflat = jnp.tile(d_skip.astype(_F32).reshape(N_GROUPS, 1, SSM_GROUP), (1, t, 1))
    dflat = dflat.reshape(N_GROUPS, 1, CHUNK_LANES)
    return w, bm, e, coef, dflat


def _ssm_kernel(u_ref, w_ref, bm_ref, e_ref, coef_ref, d_ref, o_ref, hc_ref, *,
                n_seq, n_chunks, row_block):
    rows = n_seq * n_chunks
    sl = STATE_LANES
    s = jnp.dot(u_ref[...], bm_ref[...], preferred_element_type=_F32)
    row = lax.broadcasted_iota(jnp.int32, (rows, sl), 0)

    def shift_rows(x, n, down):
        if down:
            return jnp.where(row >= n, pltpu.roll(x, n, 0), 0.0)
        return jnp.where(row < rows - n, pltpu.roll(x, rows - n, 0), 0.0)

    def cmul(x, a1, a2):
        return x * a1 + pltpu.roll(x, SSM_STATE, 1) * a2

    xf = s[:, :sl]
    xb = s[:, sl:]
    level = 0
    d = 1
    while d < n_chunks:
        a1 = coef_ref[level:level + 1, :]
        a2 = coef_ref[N_SCAN_LEVELS + level:N_SCAN_LEVELS + level + 1, :]
        xf = xf + cmul(shift_rows(xf, d * n_seq, True), a1[:, :sl], a2[:, :sl])
        xb = xb + cmul(shift_rows(xb, d * n_seq, False), a1[:, sl:], a2[:, sl:])
        d *= 2
        level += 1
    hc_ref[:, :sl] = shift_rows(xf, n_seq, True).astype(_BF16)
    hc_ref[:, sl:] = shift_rows(xb, n_seq, False).astype(_BF16)

    for r0 in range(0, rows, row_block):
        u = u_ref[r0:r0 + row_block, :]
        y = jnp.dot(u, w_ref[...], preferred_element_type=_F32)
        y = y + jnp.dot(hc_ref[r0:r0 + row_block, :], e_ref[...],
                        preferred_element_type=_F32)
        y = y + d_ref[...] * u.astype(_F32)
        o_ref[r0:r0 + row_block, :] = jax.nn.gelu(y, approximate=True).astype(o_ref.dtype)


def _ssm_mixer(zs, ops):
    w, bm, e, coef, dflat = ops
    b, l, _ = zs.shape
    t = SSM_CHUNK
    nch = l // t
    rows = b * nch
    u = zs.reshape(b, nch, t, N_GROUPS, SSM_GROUP)
    u = jnp.transpose(u, (3, 1, 0, 2, 4)).reshape(N_GROUPS, rows, CHUNK_LANES)
    row_block = min(rows, 256)
    out = pl.pallas_call(
        functools.partial(_ssm_kernel, n_seq=b, n_chunks=nch, row_block=row_block),
        grid=(N_GROUPS,),
        in_specs=[
            pl.BlockSpec((None, rows, CHUNK_LANES), lambda g: (g, 0, 0)),
            pl.BlockSpec((None, CHUNK_LANES, CHUNK_LANES), lambda g: (g, 0, 0)),
            pl.BlockSpec((None, CHUNK_LANES, 2 * STATE_LANES), lambda g: (g, 0, 0)),
            pl.BlockSpec((None, 2 * STATE_LANES, CHUNK_LANES), lambda g: (g, 0, 0)),
            pl.BlockSpec((None, 2 * N_SCAN_LEVELS, 2 * STATE_LANES), lambda g: (g, 0, 0)),
            pl.BlockSpec((None, 1, CHUNK_LANES), lambda g: (g, 0, 0)),
        ],
        out_specs=pl.BlockSpec((None, rows, CHUNK_LANES), lambda g: (g, 0, 0)),
        out_shape=jax.ShapeDtypeStruct((N_GROUPS, rows, CHUNK_LANES), _BF16),
        scratch_shapes=[pltpu.VMEM((rows, 2 * STATE_LANES), _BF16)],
        name="ssm_chunked",
    )(u, w, bm, e, coef, dflat)
    out = out.reshape(N_GROUPS, nch, b, t, SSM_GROUP)
    return jnp.transpose(out, (2, 1, 3, 0, 4)).reshape(b, l, D_SSM)


def _tail_kernel(x_ref, yf_ref, gs_ref, mod_ref, wglu_ref, bglu_ref, gf_ref, gss_ref,
                 wout_ref, gmlp_ref, w1_ref, b1_ref, w2_ref, b2_ref, gfin_ref, o_ref):
    gate1 = mod_ref[2:3, :]
    shift2 = mod_ref[3:4, :]
    scale2 = mod_ref[4:5, :]
    gate2 = mod_ref[5:6, :]

    ab = jnp.dot(gs_ref[...], wglu_ref[...], preferred_element_type=_F32) + bglu_ref[...]
    ys = ab[:, :D_SSM] * jax.nn.sigmoid(ab[:, D_SSM:])
    ysn = _rms(ys) * gss_ref[...]
    yfn = _rms(yf_ref[...].astype(_F32)) * gf_ref[...]
    merged = jnp.concatenate([yfn, ysn], axis=-1).astype(_BF16)
    x1 = x_ref[...] + gate1 * jnp.dot(merged, wout_ref[...], preferred_element_type=_F32)

    h2 = (_rms(x1) * gmlp_ref[...] * (1.0 + scale2) + shift2).astype(_BF16)
    ff = jnp.zeros(x1.shape, _F32)
    for c0 in range(0, D_FF, FF_CHUNK):
        a = jnp.dot(h2, w1_ref[:, c0:c0 + FF_CHUNK], preferred_element_type=_F32)
        a = jnp.maximum(a + b1_ref[:, c0:c0 + FF_CHUNK], 0.0)
        a = (a * a).astype(_BF16)
        ff = ff + jnp.dot(a, w2_ref[c0:c0 + FF_CHUNK, :], preferred_element_type=_F32)
    x2 = x1 + gate2 * (ff + b2_ref[...])
    o_ref[...] = _rms(x2) * gfin_ref[...]


def _tail(x, yf, gs, mod, wglu, b_glu, g_f, g_s, wout, g_mlp, w1, b1, w2, b2, g_final):
    b, l, _ = x.shape
    tm = TOKEN_TILE
    tok = lambda width: pl.BlockSpec((None, tm, width), lambda i, j: (i, j, 0))
    row = lambda a: a.reshape(1, -1)
    single = lambda shape: pl.BlockSpec(shape, lambda *_: (0,) * len(shape),
                                        pipeline_mode=pl.Buffered(1))
    return pl.pallas_call(
        _tail_kernel,
        grid=(b, l // tm),
        in_specs=[
            tok(D_MODEL), tok(D_FOURIER), tok(D_SSM),
            pl.BlockSpec((None, N_MOD, D_MODEL), lambda i, j: (i, 0, 0)),
            single((D_SSM, 2 * D_SSM)), single((1, 2 * D_SSM)),
            single((1, D_FOURIER)), single((1, D_SSM)),
            single((D_MODEL, D_MODEL)), single((1, D_MODEL)),
            single((D_MODEL, D_FF)), single((1, D_FF)),
            single((D_FF, D_MODEL)), single((1, D_MODEL)), single((1, D_MODEL)),
        ],
        out_specs=tok(D_MODEL),
        out_shape=jax.ShapeDtypeStruct((b, l, D_MODEL), _F32),
        compiler_params=pltpu.CompilerParams(vmem_limit_bytes=VMEM_LIMIT),
        name="tail",
    )(x, yf, gs, mod, wglu, row(b_glu), row(g_f), row(g_s), wout, row(g_mlp),
      w1, row(b1), w2, row(b2), row(g_final))


def kernel(x_prompt, x_sample, c_prompt, c_sample, w_ada, b_ada, g_mix_norm, w_in, w_fourier, b_fourier, ssm_a_re, ssm_a_im, ssm_log_dt, ssm_b_re, ssm_b_im, ssm_c_re, ssm_c_im, ssm_d, w_glu, b_glu, g_fourier_out, g_ssm_out, w_out, g_mlp_norm, w_mlp_in, b_mlp_in, w_mlp_out, b_mlp_out, g_final):
    assert w_ada.shape[0] == 1, "single-layer block"
    n_p = c_prompt.shape[0]
    n_s = c_sample.shape[0]
    pad = (-(n_p + n_s)) % 8
    c_all = jnp.concatenate(
        [c_prompt, c_sample, jnp.zeros((pad, D_MODEL), c_prompt.dtype)], axis=0)
    mod = _adaln(c_all, w_ada[0], b_ada[0]).reshape(-1, N_MOD, D_MODEL)

    w_in_b = w_in[0].astype(_BF16)
    wcomb = _fold_fourier_weight(w_fourier[0])
    ssm_ops = _ssm_operators(ssm_a_re[0], ssm_a_im[0], ssm_log_dt[0], ssm_b_re[0],
                             ssm_b_im[0], ssm_c_re[0], ssm_c_im[0], ssm_d[0])
    wglu_b = w_glu[0].astype(_BF16)
    wout_b = w_out[0].astype(_BF16)
    w1_b = w_mlp_in[0].astype(_BF16)
    w2_b = w_mlp_out[0].astype(_BF16)

    def trunk(x, m):
        zf, zs = _inproj(x, m, g_mix_norm[0], w_in_b)
        yf = _fourier_mixer(zf, wcomb, b_fourier[0])
        gs = _ssm_mixer(zs, ssm_ops)
        return _tail(x, yf, gs, m, wglu_b, b_glu[0], g_fourier_out[0], g_ssm_out[0],
                     wout_b, g_mlp_norm[0], w1_b, b_mlp_in[0], w2_b, b_mlp_out[0], g_final)

    y_prompt = trunk(x_prompt, mod[:n_p])
    y_sample = trunk(x_sample, mod[n_p:n_p + n_s])
    return (y_prompt, y_sample)
```

```python
import functools
import math

import numpy as np
import jax
import jax.numpy as jnp
from jax import lax
from jax.experimental import pallas as pl
from jax.experimental.pallas import tpu as pltpu

D_MODEL = 1024
D_FOURIER = 512
D_SSM = 512
N_HEADS = 4
HEAD_DIM = 128
SSM_GROUP = 16
N_GROUPS = 32
SSM_STATE = 64
D_FF = 4096
N_MOD = 6
EPS = 1e-6
A_RE_MAX = -1e-4

SSM_CHUNK = 128
CHUNK_LANES = SSM_CHUNK * SSM_GROUP
STATE_LANES = 2 * SSM_STATE
DFT_N1 = 128
TOKEN_TILE = 512
FF_CHUNK = 1024
VMEM_LIMIT = 56 * 1024 * 1024

_BF16 = jnp.bfloat16
_F32 = jnp.float32


def _rms(v):
    return v * lax.rsqrt(jnp.mean(v * v, axis=-1, keepdims=True) + EPS)


def _const_spec(shape):
    nd = len(shape)
    return pl.BlockSpec(shape, lambda *_: (0,) * nd)


def _adaln_kernel(c_ref, w_ref, b_ref, o_ref):
    c = c_ref[...]
    s = c * jax.nn.sigmoid(c)
    o_ref[...] = jnp.dot(s, w_ref[...], preferred_element_type=_F32,
                         precision=lax.Precision.HIGHEST) + b_ref[...]


def _adaln(c_all, w_ada, b_ada):
    rows = c_all.shape[0]
    return pl.pallas_call(
        _adaln_kernel,
        grid=(N_MOD,),
        in_specs=[
            _const_spec((rows, D_MODEL)),
            pl.BlockSpec((D_MODEL, D_MODEL), lambda j: (0, j)),
            pl.BlockSpec((1, D_MODEL), lambda j: (0, j)),
        ],
        out_specs=pl.BlockSpec((rows, D_MODEL), lambda j: (0, j)),
        out_shape=jax.ShapeDtypeStruct((rows, N_MOD * D_MODEL), _F32),
        name="adaln",
    )(c_all, w_ada, b_ada.reshape(1, -1))


def _fold_kernel(cs_ref, wf_ref, o_ref):
    o_ref[...] = jnp.dot(cs_ref[...], wf_ref[...], preferred_element_type=_F32,
                         precision=lax.Precision.HIGHEST).astype(o_ref.dtype)


@functools.lru_cache(maxsize=None)
def _channel_dft_np():
    k = np.arange(D_FOURIER)
    ang = 2.0 * np.pi * ((k[:, None] * k[None, :]) % D_FOURIER) / D_FOURIER
    scale = 1.0 / math.sqrt(D_FOURIER)
    return np.concatenate([np.cos(ang), np.sin(ang)], axis=0).astype(np.float32) * scale


def _fold_fourier_weight(w_fourier):
    wblk = jnp.zeros((D_FOURIER, D_FOURIER), _F32)
    for h in range(N_HEADS):
        sl = slice(h * HEAD_DIM, (h + 1) * HEAD_DIM)
        wblk = wblk.at[sl, sl].set(w_fourier[h])
    cs = jnp.asarray(_channel_dft_np())
    return pl.pallas_call(
        _fold_kernel,
        out_shape=jax.ShapeDtypeStruct((2 * D_FOURIER, D_FOURIER), _BF16),
        name="fold_fourier_weight",
    )(cs, wblk)


def _inproj_kernel(x_ref, mod_ref, g_ref, w_ref, zf_ref, zs_ref):
    x = x_ref[...]
    shift = mod_ref[0:1, :]
    scale = mod_ref[1:2, :]
    h = _rms(x) * g_ref[...] * (1.0 + scale) + shift
    z = jnp.dot(h.astype(_BF16), w_ref[...], preferred_element_type=_F32)
    zf_ref[...] = z[:, :D_FOURIER].astype(_BF16)
    zs_ref[...] = z[:, D_FOURIER:].astype(_BF16)


def _inproj(x, mod, g_mix, w_in_bf16):
    b, l, _ = x.shape
    tm = TOKEN_TILE
    return pl.pallas_call(
        _inproj_kernel,
        grid=(b, l // tm),
        in_specs=[
            pl.BlockSpec((None, tm, D_MODEL), lambda i, j: (i, j, 0)),
            pl.BlockSpec((None, N_MOD, D_MODEL), lambda i, j: (i, 0, 0)),
            _const_spec((1, D_MODEL)),
            _const_spec((D_MODEL, D_MODEL)),
        ],
        out_specs=[
            pl.BlockSpec((None, tm, D_FOURIER), lambda i, j: (i, j, 0)),
            pl.BlockSpec((None, tm, D_SSM), lambda i, j: (i, j, 0)),
        ],
        out_shape=[
            jax.ShapeDtypeStruct((b, l, D_FOURIER), _BF16),
            jax.ShapeDtypeStruct((b, l, D_SSM), _BF16),
        ],
        name="inproj",
    )(x, mod, g_mix.reshape(1, -1), w_in_bf16)


@functools.lru_cache(maxsize=None)
def _dft_tables_np(n):
    n1 = DFT_N1
    n2 = n // n1
    i1 = np.arange(n1)
    ang1 = 2.0 * np.pi * ((i1[:, None] * i1[None, :]) % n1) / n1
    f1 = np.concatenate([np.cos(ang1), -np.sin(ang1)], axis=0).astype(np.float32)
    i2 = np.arange(n2)
    m = (i2[None, None, :] * (i1[:, None, None] + n1 * i2[None, :, None])) % n
    ang = 2.0 * np.pi * m / n
    gr = np.cos(ang) / math.sqrt(n)
    gi = -np.sin(ang) / math.sqrt(n)
    g = np.concatenate([np.concatenate([gr, -gi], axis=2),
                        np.concatenate([gi, gr], axis=2)], axis=1).astype(np.float32)
    return f1, g


def _dft1_kernel(f_ref, x_ref, a_ref):
    a_ref[...] = jnp.dot(f_ref[...], x_ref[...],
                         preferred_element_type=_F32).astype(_BF16)


def _dft_stage1(zf, f1):
    b, l, _ = zf.shape
    n1 = DFT_N1
    cols = (l // n1) * D_FOURIER
    cw = 4096
    x2d = zf.reshape(b, n1, cols)
    return pl.pallas_call(
        _dft1_kernel,
        grid=(b, cols // cw),
        in_specs=[
            _const_spec((2 * n1, n1)),
            pl.BlockSpec((None, n1, cw), lambda i, j: (i, 0, j)),
        ],
        out_specs=pl.BlockSpec((None, 2 * n1, cw), lambda i, j: (i, 0, j)),
        out_shape=jax.ShapeDtypeStruct((b, 2 * n1, cols), _BF16),
        name="dft_stage1",
    )(f1, x2d)


def _dft2_kernel(a_ref, g_ref, wc_ref, bf_ref, o_ref, *, kb, n2):
    xr, xi = [], []
    for j in range(kb):
        aa = jnp.concatenate([a_ref[0, j], a_ref[1, j]], axis=0)
        x = jnp.dot(g_ref[j], aa, preferred_element_type=_F32)
        xr.append(x[:n2].astype(_BF16))
        xi.append(x[n2:].astype(_BF16))
    xr = jnp.concatenate(xr, axis=0)
    xi = jnp.concatenate(xi, axis=0)
    y = jnp.dot(xr, wc_ref[:D_FOURIER, :], preferred_element_type=_F32)
    y = y + jnp.dot(xi, wc_ref[D_FOURIER:, :], preferred_element_type=_F32)
    y = y + bf_ref[...]
    o_ref[...] = y.reshape(kb, n2, D_FOURIER).astype(o_ref.dtype)


def _dft_stage2(a, g, wcomb, b_fourier):
    b = a.shape[0]
    n1 = DFT_N1
    n2 = g.shape[1] // 2
    kb = max(8, 1024 // n2)
    a5 = a.reshape(b, 2, n1, n2, D_FOURIER)
    return pl.pallas_call(
        functools.partial(_dft2_kernel, kb=kb, n2=n2),
        grid=(b, n1 // kb),
        in_specs=[
            pl.BlockSpec((None, 2, kb, n2, D_FOURIER), lambda i, j: (i, 0, j, 0, 0)),
            pl.BlockSpec((kb, 2 * n2, 2 * n2), lambda i, j: (j, 0, 0)),
            _const_spec((2 * D_FOURIER, D_FOURIER)),
            _const_spec((1, D_FOURIER)),
        ],
        out_specs=pl.BlockSpec((None, kb, n2, D_FOURIER), lambda i, j: (i, j, 0, 0)),
        out_shape=jax.ShapeDtypeStruct((b, n1, n2, D_FOURIER), _BF16),
        name="dft_stage2",
    )(a5, g, wcomb, b_fourier.reshape(1, -1))


def _fourier_mixer(zf, wcomb, b_fourier):
    b, l, _ = zf.shape
    f1_np, g_np = _dft_tables_np(l)
    f1 = jnp.asarray(f1_np).astype(_BF16)
    g = jnp.asarray(g_np).astype(_BF16)
    a = _dft_stage1(zf, f1)
    y = _dft_stage2(a, g, wcomb, b_fourier)
    return jnp.swapaxes(y, 1, 2).reshape(b, l, D_FOURIER)


def _cpow(k, zr, zi):
    mag = jnp.exp(k * zr)
    return mag * jnp.cos(k * zi), mag * jnp.sin(k * zi)


def _discretise(a_re, a_im, log_dt):
    lam_re = jnp.minimum(a_re, A_RE_MAX)
    dt = jnp.exp(log_dt)
    return lam_re, a_im, lam_re * dt, a_im * dt


def _ssm_kernel(arow_ref, acol_ref, bt_ref, ct_ref, d_ref, up_ref, us_ref,
                op_ref, os_ref, w_scr, bm_scr, e_scr, lag_scr, bb_scr,
                hcp_scr, hcs_scr, *, chunks_p, chunks_s):
    t = SSM_CHUNK
    p = SSM_STATE
    hi = lax.Precision.HIGHEST
    lane_k = lax.broadcasted_iota(jnp.int32, (p, t), 1).astype(_F32)
    sub_k = lax.broadcasted_iota(jnp.int32, (t, p), 0).astype(_F32)
    lane0 = lax.broadcasted_iota(jnp.int32, (p, t), 1) == 0

    zrow, bb = [], []
    for d in range(2):
        lam_re, lam_im, zr, zi = _discretise(arow_ref[d, 0:1, :], arow_ref[d, 1:2, :],
                                             arow_ref[d, 2:3, :])
        th = jnp.tanh(0.5 * zr)
        em1 = 2.0 * th / (1.0 - th)
        e_re = em1 * jnp.cos(zi) - 2.0 * jnp.square(jnp.sin(0.5 * zi))
        e_im = (em1 + 1.0) * jnp.sin(zi)
        den = lam_re * lam_re + lam_im * lam_im
        q_re = (e_re * lam_re + e_im * lam_im) / den
        q_im = (e_im * lam_re - e_re * lam_im) / den
        b_re = bt_ref[d, 0]
        b_im = bt_ref[d, 1]
        bb.append((q_re * b_re - q_im * b_im, q_re * b_im + q_im * b_re))
        zrow.append((zr, zi))
    for hp in range(SSM_GROUP):
        for d in range(2):
            for part in range(2):
                bb_scr[2 * d + part, hp] = bb[d][part][hp:hp + 1, :]

    def c_times(d, pw_re, pw_im):
        re, im = [], []
        for h in range(SSM_GROUP):
            c_re = jnp.broadcast_to(ct_ref[d, 0][:, h:h + 1], (p, t))
            c_im = jnp.broadcast_to(ct_ref[d, 1][:, h:h + 1], (p, t))
            re.append(c_re * pw_re - c_im * pw_im)
            im.append(c_re * pw_im + c_im * pw_re)
        return jnp.concatenate(re, axis=1), jnp.concatenate(im, axis=1)

    _, _, zr0, zi0 = _discretise(acol_ref[0, 0], acol_ref[0, 1], acol_ref[0, 2])
    _, _, zr1, zi1 = _discretise(acol_ref[1, 0], acol_ref[1, 1], acol_ref[1, 2])
    caf_re, caf_im = c_times(0, *_cpow(lane_k, zr0, zi0))
    ca1_re, ca1_im = c_times(0, *_cpow(lane_k + 1.0, zr0, zi0))
    cab_re, cab_im = c_times(1, *_cpow(float(t) - lane_k, zr1, zi1))
    one = jnp.where(lane0, 1.0, 0.0)
    cb0_re, cb0_im = c_times(1, one, jnp.zeros_like(one))

    e_scr[0 * p:1 * p, :] = ca1_re.astype(_BF16)
    e_scr[1 * p:2 * p, :] = (-ca1_im).astype(_BF16)
    e_scr[2 * p:3 * p, :] = cab_re.astype(_BF16)
    e_scr[3 * p:4 * p, :] = (-cab_im).astype(_BF16)

    (bbf_re, bbf_im), (bbb_re, bbb_im) = bb
    lhs_f = jnp.concatenate([bbf_re, -bbf_im, bbb_re, -bbb_im], axis=1)
    rhs_f = jnp.concatenate([caf_re, caf_im, cb0_re, cb0_im], axis=0)
    kf = jnp.dot(lhs_f, rhs_f, preferred_element_type=_F32, precision=hi)
    lhs_b = jnp.concatenate([bbb_re, -bbb_im], axis=1)
    rhs_b = jnp.concatenate([cab_re, cab_im], axis=0)
    kb = jnp.dot(lhs_b, rhs_b, preferred_element_type=_F32, precision=hi)
    for hp in range(SSM_GROUP):
        lag_scr[hp, :, :CHUNK_LANES] = kf[hp:hp + 1, :]
        lag_scr[hp, :, CHUNK_LANES:] = kb[hp:hp + 1, :]

    pf_re, pf_im = _cpow(float(t - 1) - sub_k, *zrow[0])
    pb_re, pb_im = _cpow(sub_k, *zrow[1])

    def build_rows(hp, carry):
        r0 = pl.multiple_of(hp * t, t)
        lags = lag_scr[hp]
        for h in range(SSM_GROUP):
            lag = jnp.concatenate([lags[:, h * t:(h + 1) * t],
                                   lags[:, CHUNK_LANES + h * t:CHUNK_LANES + (h + 1) * t]],
                                  axis=1)
            skew = pltpu.roll(jnp.broadcast_to(lag, (t, 2 * t)), 0, 1,
                              stride=1, stride_axis=0)
            w_scr[pl.ds(r0, t), h * t:(h + 1) * t] = skew[:, :t].astype(_BF16)
        f_re = bb_scr[0, hp]
        f_im = bb_scr[1, hp]
        g_re = bb_scr[2, hp]
        g_im = bb_scr[3, hp]
        blk = jnp.concatenate([pf_re * f_re - pf_im * f_im, pf_re * f_im + pf_im * f_re,
                               pb_re * g_re - pb_im * g_im, pb_re * g_im + pb_im * g_re],
                              axis=1)
        bm_scr[pl.ds(r0, t), :] = blk.astype(_BF16)
        return carry

    lax.fori_loop(0, SSM_GROUP, build_rows, 0)

    sl = STATE_LANES

    def run(u_ref, o_ref, hc_scr, n_chunks):
        rows = u_ref.shape[0]
        s = jnp.dot(u_ref[...], bm_scr[...], preferred_element_type=_F32)
        pos = lax.broadcasted_iota(jnp.int32, (rows, sl), 0) & (n_chunks - 1)

        def shift_rows(x, n, down):
            if down:
                return jnp.where(pos >= n, pltpu.roll(x, n, 0), 0.0)
            return jnp.where(pos < n_chunks - n, pltpu.roll(x, rows - n, 0), 0.0)

        def cmul(x, ar, ai):
            a1 = jnp.concatenate([ar, ar], axis=1)
            a2 = jnp.concatenate([-ai, ai], axis=1)
            return x * a1 + pltpu.roll(x, SSM_STATE, 1) * a2

        xf = s[:, :sl]
        xb = s[:, sl:]
        dist = 1
        while dist < n_chunks:
            kk = float(t * dist)
            xf = xf + cmul(shift_rows(xf, dist, True), *_cpow(kk, *zrow[0]))
            xb = xb + cmul(shift_rows(xb, dist, False), *_cpow(kk, *zrow[1]))
            dist *= 2
        hc_scr[:, :sl] = shift_rows(xf, 1, True).astype(_BF16)
        hc_scr[:, sl:] = shift_rows(xb, 1, False).astype(_BF16)

        rb = min(rows, 256)
        for r0 in range(0, rows, rb):
            u = u_ref[r0:r0 + rb, :]
            y = jnp.dot(u, w_scr[...], preferred_element_type=_F32)
            y = y + jnp.dot(hc_scr[r0:r0 + rb, :], e_scr[...], preferred_element_type=_F32)
            y = y + d_ref[...] * u.astype(_F32)
            o_ref[r0:r0 + rb, :] = jax.nn.gelu(y, approximate=True).astype(o_ref.dtype)

    run(up_ref, op_ref, hcp_scr, chunks_p)
    run(us_ref, os_ref, hcs_scr, chunks_s)


def _to_chunks(zs):
    b, l, _ = zs.shape
    nch = l // SSM_CHUNK
    u = zs.reshape(b, nch, SSM_CHUNK, N_GROUPS, SSM_GROUP)
    return jnp.transpose(u, (3, 0, 1, 4, 2)).reshape(N_GROUPS, b * nch, CHUNK_LANES)


def _from_chunks(y, b, l):
    nch = l // SSM_CHUNK
    y = y.reshape(N_GROUPS, b, nch, SSM_GROUP, SSM_CHUNK)
    return jnp.transpose(y, (1, 2, 4, 0, 3)).reshape(b, l, D_SSM)


def _ssm_mixer(zs_p, zs_s, a_re, a_im, log_dt, b_re, b_im, c_re, c_im, d_skip):
    g, p, h, t = N_GROUPS, SSM_STATE, SSM_GROUP, SSM_CHUNK
    ldt = jnp.broadcast_to(log_dt[:, :, None], (2, g, p))
    arow = jnp.stack([a_re, a_im, ldt] + [jnp.zeros_like(a_re)] * 5, axis=2)
    arow = jnp.transpose(arow, (1, 0, 2, 3)).astype(_F32)
    acol = jnp.stack([a_re, a_im, ldt], axis=2)
    acol = jnp.transpose(acol, (1, 0, 2, 3)).astype(_F32)
    acol = jnp.broadcast_to(acol[..., None], (g, 2, 3, p, t))
    bt = jnp.transpose(jnp.stack([b_re, b_im], axis=2), (1, 0, 2, 4, 3)).astype(_F32)
    ct = jnp.transpose(jnp.stack([c_re, c_im], axis=2), (1, 0, 2, 4, 3)).astype(_F32)
    dfl = jnp.repeat(d_skip.astype(_F32).reshape(g, 1, h), t, axis=2)

    bp, lp, _ = zs_p.shape
    bs, ls, _ = zs_s.shape
    up = _to_chunks(zs_p)
    us = _to_chunks(zs_s)
    rows_p, rows_s = up.shape[1], us.shape[1]
    grp = lambda *tail: pl.BlockSpec((None,) + tail, lambda i: (i,) + (0,) * len(tail))
    op, os_ = pl.pallas_call(
        functools.partial(_ssm_kernel, chunks_p=lp // t, chunks_s=ls // t),
        grid=(g,),
        in_specs=[grp(2, 8, p), grp(2, 3, p, t), grp(2, 2, h, p), grp(2, 2, p, h),
                  grp(1, CHUNK_LANES), grp(rows_p, CHUNK_LANES), grp(rows_s, CHUNK_LANES)],
        out_specs=[grp(rows_p, CHUNK_LANES), grp(rows_s, CHUNK_LANES)],
        out_shape=[jax.ShapeDtypeStruct((g, rows_p, CHUNK_LANES), _BF16),
                   jax.ShapeDtypeStruct((g, rows_s, CHUNK_LANES), _BF16)],
        scratch_shapes=[
            pltpu.VMEM((CHUNK_LANES, CHUNK_LANES), _BF16),
            pltpu.VMEM((CHUNK_LANES, 2 * STATE_LANES), _BF16),
            pltpu.VMEM((2 * STATE_LANES, CHUNK_LANES), _BF16),
            pltpu.VMEM((h, 1, 2 * CHUNK_LANES), _F32),
            pltpu.VMEM((4, h, 1, p), _F32),
            pltpu.VMEM((rows_p, 2 * STATE_LANES), _BF16),
            pltpu.VMEM((rows_s, 2 * STATE_LANES), _BF16),
        ],
        compiler_params=pltpu.CompilerParams(vmem_limit_bytes=VMEM_LIMIT),
        name="ssm_chunked",
    )(arow, acol, bt, ct, dfl, up, us)
    return _from_chunks(op, bp, lp), _from_chunks(os_, bs, ls)


def _tail_kernel(x_ref, yf_ref, gs_ref, mod_ref, wglu_ref, bglu_ref, gf_ref, gss_ref,
                 wout_ref, gmlp_ref, w1_ref, b1_ref, w2_ref, b2_ref, gfin_ref, o_ref):
    gate1 = mod_ref[2:3, :]
    shift2 = mod_ref[3:4, :]
    scale2 = mod_ref[4:5, :]
    gate2 = mod_ref[5:6, :]

    ab = jnp.dot(gs_ref[...], wglu_ref[...], preferred_element_type=_F32) + bglu_ref[...]
    ys = ab[:, :D_SSM] * jax.nn.sigmoid(ab[:, D_SSM:])
    ysn = _rms(ys) * gss_ref[...]
    yfn = _rms(yf_ref[...].astype(_F32)) * gf_ref[...]
    merged = jnp.concatenate([yfn, ysn], axis=-1).astype(_BF16)
    x1 = x_ref[...] + gate1 * jnp.dot(merged, wout_ref[...], preferred_element_type=_F32)

    h2 = (_rms(x1) * gmlp_ref[...] * (1.0 + scale2) + shift2).astype(_BF16)
    ff = jnp.zeros(x1.shape, _F32)
    for c0 in range(0, D_FF, FF_CHUNK):
        a = jnp.dot(h2, w1_ref[:, c0:c0 + FF_CHUNK], preferred_element_type=_F32)
        a = jnp.maximum(a + b1_ref[:, c0:c0 + FF_CHUNK], 0.0)
        a = (a * a).astype(_BF16)
        ff = ff + jnp.dot(a, w2_ref[c0:c0 + FF_CHUNK, :], preferred_element_type=_F32)
    x2 = x1 + gate2 * (ff + b2_ref[...])
    o_ref[...] = _rms(x2) * gfin_ref[...]


def _tail(x, yf, gs, mod, wglu, b_glu, g_f, g_s, wout, g_mlp, w1, b1, w2, b2, g_final):
    b, l, _ = x.shape
    tm = TOKEN_TILE
    tok = lambda width: pl.BlockSpec((None, tm, width), lambda i, j: (i, j, 0))
    row = lambda a: a.reshape(1, -1)
    single = lambda shape: pl.BlockSpec(shape, lambda *_: (0,) * len(shape),
                                        pipeline_mode=pl.Buffered(1))
    return pl.pallas_call(
        _tail_kernel,
        grid=(b, l // tm),
        in_specs=[
            tok(D_MODEL), tok(D_FOURIER), tok(D_SSM),
            pl.BlockSpec((None, N_MOD, D_MODEL), lambda i, j: (i, 0, 0)),
            single((D_SSM, 2 * D_SSM)), single((1, 2 * D_SSM)),
            single((1, D_FOURIER)), single((1, D_SSM)),
            single((D_MODEL, D_MODEL)), single((1, D_MODEL)),
            single((D_MODEL, D_FF)), single((1, D_FF)),
            single((D_FF, D_MODEL)), single((1, D_MODEL)), single((1, D_MODEL)),
        ],
        out_specs=tok(D_MODEL),
        out_shape=jax.ShapeDtypeStruct((b, l, D_MODEL), _F32),
        compiler_params=pltpu.CompilerParams(vmem_limit_bytes=VMEM_LIMIT),
        name="tail",
    )(x, yf, gs, mod, wglu, row(b_glu), row(g_f), row(g_s), wout, row(g_mlp),
      w1, row(b1), w2, row(b2), row(g_final))


def kernel(x_prompt, x_sample, c_prompt, c_sample, w_ada, b_ada, g_mix_norm, w_in, w_fourier, b_fourier, ssm_a_re, ssm_a_im, ssm_log_dt, ssm_b_re, ssm_b_im, ssm_c_re, ssm_c_im, ssm_d, w_glu, b_glu, g_fourier_out, g_ssm_out, w_out, g_mlp_norm, w_mlp_in, b_mlp_in, w_mlp_out, b_mlp_out, g_final):
    assert w_ada.shape[0] == 1, "single-layer block"
    n_p = c_prompt.shape[0]
    n_s = c_sample.shape[0]
    pad = (-(n_p + n_s)) % 8
    c_all = jnp.concatenate(
        [c_prompt, c_sample, jnp.zeros((pad, D_MODEL), c_prompt.dtype)], axis=0)
    mod = _adaln(c_all, w_ada[0], b_ada[0]).reshape(-1, N_MOD, D_MODEL)

    w_in_b = w_in[0].astype(_BF16)
    wcomb = _fold_fourier_weight(w_fourier[0])
    wglu_b = w_glu[0].astype(_BF16)
    wout_b = w_out[0].astype(_BF16)
    w1_b = w_mlp_in[0].astype(_BF16)
    w2_b = w_mlp_out[0].astype(_BF16)
    mod_p = mod[:n_p]
    mod_s = mod[n_p:n_p + n_s]

    zf_p, zs_p = _inproj(x_prompt, mod_p, g_mix_norm[0], w_in_b)
    zf_s, zs_s = _inproj(x_sample, mod_s, g_mix_norm[0], w_in_b)
    gs_p, gs_s = _ssm_mixer(zs_p, zs_s, ssm_a_re[0], ssm_a_im[0], ssm_log_dt[0], ssm_b_re[0],
                            ssm_b_im[0], ssm_c_re[0], ssm_c_im[0], ssm_d[0])

    def finish(x, m, zf, gs):
        yf = _fourier_mixer(zf, wcomb, b_fourier[0])
        return _tail(x, yf, gs, m, wglu_b, b_glu[0], g_fourier_out[0], g_ssm_out[0],
                     wout_b, g_mlp_norm[0], w1_b, b_mlp_in[0], w2_b, b_mlp_out[0], g_final)

    return (finish(x_prompt, mod_p, zf_p, gs_p), finish(x_sample, mod_s, zf_s, gs_s))
```

```python
import functools
import math

import numpy as np
import jax
import jax.numpy as jnp
from jax import lax
from jax.experimental import pallas as pl
from jax.experimental.pallas import tpu as pltpu

D_MODEL = 1024
D_FOURIER = 512
D_SSM = 512
N_HEADS = 4
HEAD_DIM = 128
SSM_GROUP = 16
N_GROUPS = 32
SSM_STATE = 64
D_FF = 4096
N_MOD = 6
EPS = 1e-6
A_RE_MAX = -1e-4

SSM_CHUNK = 128
CHUNK_LANES = SSM_CHUNK * SSM_GROUP
STATE_LANES = 2 * SSM_STATE
DFT_N1 = 128
LANES = 128
SUBLANES = 8
ROW_PAD = 8
ZS_PITCH = D_SSM + ROW_PAD
INPROJ_TILE = 1024
TOKEN_TILE = 512
TAIL_GROUP = INPROJ_TILE // TOKEN_TILE
FF_CHUNK = 1024
VMEM_LIMIT = 56 * 1024 * 1024

_BF16 = jnp.bfloat16
_F32 = jnp.float32


def _rms(v):
    return v * lax.rsqrt(jnp.mean(v * v, axis=-1, keepdims=True) + EPS)


def _const_spec(shape):
    nd = len(shape)
    return pl.BlockSpec(shape, lambda *_: (0,) * nd)


def _adaln_kernel(c_ref, w_ref, b_ref, o_ref):
    c = c_ref[...]
    s = c * jax.nn.sigmoid(c)
    o_ref[...] = jnp.dot(s, w_ref[...], preferred_element_type=_F32,
                         precision=lax.Precision.HIGHEST) + b_ref[...]


def _adaln(c_all, w_ada, b_ada):
    rows = c_all.shape[0]
    return pl.pallas_call(
        _adaln_kernel,
        grid=(N_MOD,),
        in_specs=[
            _const_spec((rows, D_MODEL)),
            pl.BlockSpec((D_MODEL, D_MODEL), lambda j: (0, j)),
            pl.BlockSpec((1, D_MODEL), lambda j: (0, j)),
        ],
        out_specs=pl.BlockSpec((rows, D_MODEL), lambda j: (0, j)),
        out_shape=jax.ShapeDtypeStruct((rows, N_MOD * D_MODEL), _F32),
        name="adaln",
    )(c_all, w_ada, b_ada.reshape(1, -1))


def _fold_kernel(cs_ref, wf_ref, o_ref):
    o_ref[...] = jnp.dot(cs_ref[...], wf_ref[...], preferred_element_type=_F32,
                         precision=lax.Precision.HIGHEST).astype(o_ref.dtype)


@functools.lru_cache(maxsize=None)
def _channel_dft_np():
    k = np.arange(D_FOURIER)
    ang = 2.0 * np.pi * ((k[:, None] * k[None, :]) % D_FOURIER) / D_FOURIER
    scale = 1.0 / math.sqrt(D_FOURIER)
    return np.concatenate([np.cos(ang), np.sin(ang)], axis=0).astype(np.float32) * scale


def _fold_fourier_weight(w_fourier):
    wblk = jnp.zeros((D_FOURIER, D_FOURIER), _F32)
    for h in range(N_HEADS):
        sl = slice(h * HEAD_DIM, (h + 1) * HEAD_DIM)
        wblk = wblk.at[sl, sl].set(w_fourier[h])
    cs = jnp.asarray(_channel_dft_np())
    return pl.pallas_call(
        _fold_kernel,
        out_shape=jax.ShapeDtypeStruct((2 * D_FOURIER, D_FOURIER), _BF16),
        name="fold_fourier_weight",
    )(cs, wblk)


def _inproj_kernel(x_ref, mod_ref, g_ref, wf_ref, wst_ref, x2d_ref, u_ref, zf_scr, zs_scr,
                   *, n2):
    x = x_ref[...]
    tm = x.shape[0]
    h = (_rms(x) * g_ref[...] * (1.0 + mod_ref[1:2, :]) + mod_ref[0:1, :]).astype(_BF16)
    zf = jnp.dot(h, wf_ref[...], preferred_element_type=_F32)
    zst = lax.dot_general(wst_ref[...], h, (((1,), (1,)), ((), ())),
                          preferred_element_type=_F32)

    r1 = tm // n2
    pitch = n2 + ROW_PAD
    for q in range(D_FOURIER // LANES):
        for blk in range(r1):
            zf_scr[q, blk * pitch:blk * pitch + n2, :] = \
                zf[blk * n2:(blk + 1) * n2, q * LANES:(q + 1) * LANES]
    for j2 in range(n2):
        for a in range(r1 // SUBLANES):
            for q in range(D_FOURIER // LANES):
                c0 = j2 * D_FOURIER + q * LANES
                x2d_ref[SUBLANES * a:SUBLANES * (a + 1), c0:c0 + LANES] = \
                    zf_scr[q, pl.ds(j2 + SUBLANES * a * pitch, SUBLANES, stride=pitch), :]

    nck = tm // SSM_CHUNK
    for c in range(nck):
        zs_scr[c * ZS_PITCH:c * ZS_PITCH + D_SSM, :] = zst[:, c * SSM_CHUNK:(c + 1) * SSM_CHUNK]

    def per_group(g, carry):
        for hh in range(SSM_GROUP):
            u_ref[g, :, hh * SSM_CHUNK:(hh + 1) * SSM_CHUNK] = \
                zs_scr[pl.ds(g * SSM_GROUP + hh, nck, stride=ZS_PITCH), :]
        return carry

    lax.fori_loop(0, N_GROUPS, per_group, 0)


def _inproj(x, mod, g_mix, wf_bf16, wst_bf16):
    b, l, _ = x.shape
    tm = INPROJ_TILE
    n2 = l // DFT_N1
    r1 = tm // n2
    nck = tm // SSM_CHUNK
    steps = l // tm
    return pl.pallas_call(
        functools.partial(_inproj_kernel, n2=n2),
        grid=(b, steps),
        in_specs=[
            pl.BlockSpec((None, tm, D_MODEL), lambda i, j: (i, j, 0)),
            pl.BlockSpec((None, N_MOD, D_MODEL), lambda i, j: (i, 0, 0)),
            _const_spec((1, D_MODEL)),
            _const_spec((D_MODEL, D_FOURIER)),
            _const_spec((D_SSM, D_MODEL)),
        ],
        out_specs=[
            pl.BlockSpec((None, r1, n2 * D_FOURIER), lambda i, j: (i, j, 0)),
            pl.BlockSpec((N_GROUPS, nck, CHUNK_LANES), lambda i, j: (0, i * steps + j, 0)),
        ],
        out_shape=[
            jax.ShapeDtypeStruct((b, DFT_N1, n2 * D_FOURIER), _F32),
            jax.ShapeDtypeStruct((N_GROUPS, b * l // SSM_CHUNK, CHUNK_LANES), _F32),
        ],
        scratch_shapes=[
            pltpu.VMEM((D_FOURIER // LANES, r1 * (n2 + ROW_PAD), LANES), _F32),
            pltpu.VMEM((nck * ZS_PITCH, LANES), _F32),
        ],
        compiler_params=pltpu.CompilerParams(vmem_limit_bytes=VMEM_LIMIT),
        name="inproj",
    )(x, mod, g_mix.reshape(1, -1), wf_bf16, wst_bf16)


@functools.lru_cache(maxsize=None)
def _dft_tables_np(n):
    n1 = DFT_N1
    n2 = n // n1
    i1 = np.arange(n1)
    ang1 = 2.0 * np.pi * ((i1[:, None] * i1[None, :]) % n1) / n1
    f1 = np.concatenate([np.cos(ang1), -np.sin(ang1)], axis=0).astype(np.float32)
    i2 = np.arange(n2)
    m = (i2[None, None, :] * (i1[:, None, None] + n1 * i2[None, :, None])) % n
    ang = 2.0 * np.pi * m / n
    gr = np.cos(ang) / math.sqrt(n)
    gi = -np.sin(ang) / math.sqrt(n)
    g = np.concatenate([np.concatenate([gr, -gi], axis=2),
                        np.concatenate([gi, gr], axis=2)], axis=1).astype(np.float32)
    return f1, g


def _dft1_kernel(f_ref, x_ref, a_ref):
    a_ref[...] = jnp.dot(f_ref[...], x_ref[...].astype(_BF16),
                         preferred_element_type=_F32).astype(_BF16)


def _dft_stage1(x2d, f1):
    b, n1, cols = x2d.shape
    cw = 4096
    return pl.pallas_call(
        _dft1_kernel,
        grid=(b, cols // cw),
        in_specs=[
            _const_spec((2 * n1, n1)),
            pl.BlockSpec((None, n1, cw), lambda i, j: (i, 0, j)),
        ],
        out_specs=pl.BlockSpec((None, 2 * n1, cw), lambda i, j: (i, 0, j)),
        out_shape=jax.ShapeDtypeStruct((b, 2 * n1, cols), _BF16),
        name="dft_stage1",
    )(f1, x2d)


def _dft2_kernel(a_ref, g_ref, wc_ref, bf_ref, o_ref, *, kb, n2):
    xr, xi = [], []
    for j in range(kb):
        aa = jnp.concatenate([a_ref[0, j], a_ref[1, j]], axis=0)
        x = jnp.dot(g_ref[j], aa, preferred_element_type=_F32)
        xr.append(x[:n2].astype(_BF16))
        xi.append(x[n2:].astype(_BF16))
    xr = jnp.concatenate(xr, axis=0)
    xi = jnp.concatenate(xi, axis=0)
    y = jnp.dot(xr, wc_ref[:D_FOURIER, :], preferred_element_type=_F32)
    y = y + jnp.dot(xi, wc_ref[D_FOURIER:, :], preferred_element_type=_F32)
    y = y + bf_ref[...]
    o_ref[...] = y.reshape(kb, n2, D_FOURIER).astype(o_ref.dtype)


def _dft_stage2(a, g, wcomb, b_fourier):
    b = a.shape[0]
    n1 = DFT_N1
    n2 = g.shape[1] // 2
    kb = max(8, 1024 // n2)
    a5 = a.reshape(b, 2, n1, n2, D_FOURIER)
    return pl.pallas_call(
        functools.partial(_dft2_kernel, kb=kb, n2=n2),
        grid=(b, n1 // kb),
        in_specs=[
            pl.BlockSpec((None, 2, kb, n2, D_FOURIER), lambda i, j: (i, 0, j, 0, 0)),
            pl.BlockSpec((kb, 2 * n2, 2 * n2), lambda i, j: (j, 0, 0)),
            _const_spec((2 * D_FOURIER, D_FOURIER)),
            _const_spec((1, D_FOURIER)),
        ],
        out_specs=pl.BlockSpec((None, kb, n2, D_FOURIER), lambda i, j: (i, j, 0, 0)),
        out_shape=jax.ShapeDtypeStruct((b, n1, n2, D_FOURIER), _BF16),
        name="dft_stage2",
    )(a5, g, wcomb, b_fourier.reshape(1, -1))


def _fourier_mixer(x2d, wcomb, b_fourier):
    b = x2d.shape[0]
    l = x2d.shape[1] * x2d.shape[2] // D_FOURIER
    f1_np, g_np = _dft_tables_np(l)
    f1 = jnp.asarray(f1_np).astype(_BF16)
    g = jnp.asarray(g_np).astype(_BF16)
    a = _dft_stage1(x2d, f1)
    y = _dft_stage2(a, g, wcomb, b_fourier)
    return jnp.swapaxes(y, 1, 2).reshape(b, l, D_FOURIER)


def _cpow(k, zr, zi):
    mag = jnp.exp(k * zr)
    return mag * jnp.cos(k * zi), mag * jnp.sin(k * zi)


def _discretise(a_re, a_im, log_dt):
    lam_re = jnp.minimum(a_re, A_RE_MAX)
    dt = jnp.exp(log_dt)
    return lam_re, a_im, lam_re * dt, a_im * dt


def _ssm_kernel(arow_ref, acol_ref, bt_ref, ct_ref, d_ref, up_ref, us_ref,
                op_ref, os_ref, w_scr, bm_scr, e_scr, lag_scr, bb_scr,
                hcp_scr, hcs_scr, *, chunks_p, chunks_s):
    t = SSM_CHUNK
    p = SSM_STATE
    hi = lax.Precision.HIGHEST
    lane_k = lax.broadcasted_iota(jnp.int32, (p, t), 1).astype(_F32)
    sub_k = lax.broadcasted_iota(jnp.int32, (t, p), 0).astype(_F32)
    lane0 = lax.broadcasted_iota(jnp.int32, (p, t), 1) == 0

    zrow, bb = [], []
    for d in range(2):
        lam_re, lam_im, zr, zi = _discretise(arow_ref[d, 0:1, :], arow_ref[d, 1:2, :],
                                             arow_ref[d, 2:3, :])
        th = jnp.tanh(0.5 * zr)
        em1 = 2.0 * th / (1.0 - th)
        e_re = em1 * jnp.cos(zi) - 2.0 * jnp.square(jnp.sin(0.5 * zi))
        e_im = (em1 + 1.0) * jnp.sin(zi)
        den = lam_re * lam_re + lam_im * lam_im
        q_re = (e_re * lam_re + e_im * lam_im) / den
        q_im = (e_im * lam_re - e_re * lam_im) / den
        b_re = bt_ref[d, 0]
        b_im = bt_ref[d, 1]
        bb.append((q_re * b_re - q_im * b_im, q_re * b_im + q_im * b_re))
        zrow.append((zr, zi))
    for hp in range(SSM_GROUP):
        for d in range(2):
            for part in range(2):
                bb_scr[2 * d + part, hp] = bb[d][part][hp:hp + 1, :]

    def c_times(d, pw_re, pw_im):
        re, im = [], []
        for h in range(SSM_GROUP):
            c_re = jnp.broadcast_to(ct_ref[d, 0][:, h:h + 1], (p, t))
            c_im = jnp.broadcast_to(ct_ref[d, 1][:, h:h + 1], (p, t))
            re.append(c_re * pw_re - c_im * pw_im)
            im.append(c_re * pw_im + c_im * pw_re)
        return jnp.concatenate(re, axis=1), jnp.concatenate(im, axis=1)

    _, _, zr0, zi0 = _discretise(acol_ref[0, 0], acol_ref[0, 1], acol_ref[0, 2])
    _, _, zr1, zi1 = _discretise(acol_ref[1, 0], acol_ref[1, 1], acol_ref[1, 2])
    caf_re, caf_im = c_times(0, *_cpow(lane_k, zr0, zi0))
    ca1_re, ca1_im = c_times(0, *_cpow(lane_k + 1.0, zr0, zi0))
    cab_re, cab_im = c_times(1, *_cpow(float(t) - lane_k, zr1, zi1))
    one = jnp.where(lane0, 1.0, 0.0)
    cb0_re, cb0_im = c_times(1, one, jnp.zeros_like(one))

    e_scr[0 * p:1 * p, :] = ca1_re.astype(_BF16)
    e_scr[1 * p:2 * p, :] = (-ca1_im).astype(_BF16)
    e_scr[2 * p:3 * p, :] = cab_re.astype(_BF16)
    e_scr[3 * p:4 * p, :] = (-cab_im).astype(_BF16)

    (bbf_re, bbf_im), (bbb_re, bbb_im) = bb
    lhs_f = jnp.concatenate([bbf_re, -bbf_im, bbb_re, -bbb_im], axis=1)
    rhs_f = jnp.concatenate([caf_re, caf_im, cb0_re, cb0_im], axis=0)
    kf = jnp.dot(lhs_f, rhs_f, preferred_element_type=_F32, precision=hi)
    lhs_b = jnp.concatenate([bbb_re, -bbb_im], axis=1)
    rhs_b = jnp.concatenate([cab_re, cab_im], axis=0)
    kb = jnp.dot(lhs_b, rhs_b, preferred_element_type=_F32, precision=hi)
    for hp in range(SSM_GROUP):
        lag_scr[hp, :, :CHUNK_LANES] = kf[hp:hp + 1, :]
        lag_scr[hp, :, CHUNK_LANES:] = kb[hp:hp + 1, :]

    pf_re, pf_im = _cpow(float(t - 1) - sub_k, *zrow[0])
    pb_re, pb_im = _cpow(sub_k, *zrow[1])

    def build_rows(hp, carry):
        r0 = pl.multiple_of(hp * t, t)
        lags = lag_scr[hp]
        for h in range(SSM_GROUP):
            lag = jnp.concatenate([lags[:, h * t:(h + 1) * t],
                                   lags[:, CHUNK_LANES + h * t:CHUNK_LANES + (h + 1) * t]],
                                  axis=1)
            skew = pltpu.roll(jnp.broadcast_to(lag, (t, 2 * t)), 0, 1,
                              stride=1, stride_axis=0)
            w_scr[pl.ds(r0, t), h * t:(h + 1) * t] = skew[:, :t].astype(_BF16)
        f_re = bb_scr[0, hp]
        f_im = bb_scr[1, hp]
        g_re = bb_scr[2, hp]
        g_im = bb_scr[3, hp]
        blk = jnp.concatenate([pf_re * f_re - pf_im * f_im, pf_re * f_im + pf_im * f_re,
                               pb_re * g_re - pb_im * g_im, pb_re * g_im + pb_im * g_re],
                              axis=1)
        bm_scr[pl.ds(r0, t), :] = blk.astype(_BF16)
        return carry

    lax.fori_loop(0, SSM_GROUP, build_rows, 0)

    sl = STATE_LANES

    def run(u_ref, o_ref, hc_scr, n_chunks):
        rows = u_ref.shape[0]
        s = jnp.dot(u_ref[...].astype(_BF16), bm_scr[...],
                    preferred_element_type=_F32)
        pos = lax.broadcasted_iota(jnp.int32, (rows, sl), 0) & (n_chunks - 1)

        def shift_rows(x, n, down):
            if down:
                return jnp.where(pos >= n, pltpu.roll(x, n, 0), 0.0)
            return jnp.where(pos < n_chunks - n, pltpu.roll(x, rows - n, 0), 0.0)

        def cmul(x, ar, ai):
            a1 = jnp.concatenate([ar, ar], axis=1)
            a2 = jnp.concatenate([-ai, ai], axis=1)
            return x * a1 + pltpu.roll(x, SSM_STATE, 1) * a2

        xf = s[:, :sl]
        xb = s[:, sl:]
        dist = 1
        while dist < n_chunks:
            kk = float(t * dist)
            xf = xf + cmul(shift_rows(xf, dist, True), *_cpow(kk, *zrow[0]))
            xb = xb + cmul(shift_rows(xb, dist, False), *_cpow(kk, *zrow[1]))
            dist *= 2
        hc_scr[:, :sl] = shift_rows(xf, 1, True).astype(_BF16)
        hc_scr[:, sl:] = shift_rows(xb, 1, False).astype(_BF16)

        rb = min(rows, 256)
        for r0 in range(0, rows, rb):
            u = u_ref[r0:r0 + rb, :]
            y = jnp.dot(u.astype(_BF16), w_scr[...], preferred_element_type=_F32)
            y = y + jnp.dot(hc_scr[r0:r0 + rb, :], e_scr[...], preferred_element_type=_F32)
            y = y + d_ref[...] * u
            o_ref[r0:r0 + rb, :] = jax.nn.gelu(y, approximate=True).astype(o_ref.dtype)

    run(up_ref, op_ref, hcp_scr, chunks_p)
    run(us_ref, os_ref, hcs_scr, chunks_s)


def _ssm_mixer(up, us, chunks_p, chunks_s, a_re, a_im, log_dt, b_re, b_im, c_re, c_im, d_skip):
    g, p, h, t = N_GROUPS, SSM_STATE, SSM_GROUP, SSM_CHUNK
    ldt = jnp.broadcast_to(log_dt[:, :, None], (2, g, p))
    arow = jnp.stack([a_re, a_im, ldt] + [jnp.zeros_like(a_re)] * 5, axis=2)
    arow = jnp.transpose(arow, (1, 0, 2, 3)).astype(_F32)
    acol = jnp.stack([a_re, a_im, ldt], axis=2)
    acol = jnp.transpose(acol, (1, 0, 2, 3)).astype(_F32)
    acol = jnp.broadcast_to(acol[..., None], (g, 2, 3, p, t))
    bt = jnp.transpose(jnp.stack([b_re, b_im], axis=2), (1, 0, 2, 4, 3)).astype(_F32)
    ct = jnp.transpose(jnp.stack([c_re, c_im], axis=2), (1, 0, 2, 4, 3)).astype(_F32)
    dfl = jnp.repeat(d_skip.astype(_F32).reshape(g, 1, h), t, axis=2)

    rows_p, rows_s = up.shape[1], us.shape[1]
    grp = lambda *tail: pl.BlockSpec((None,) + tail, lambda i: (i,) + (0,) * len(tail))
    return pl.pallas_call(
        functools.partial(_ssm_kernel, chunks_p=chunks_p, chunks_s=chunks_s),
        grid=(g,),
        in_specs=[grp(2, 8, p), grp(2, 3, p, t), grp(2, 2, h, p), grp(2, 2, p, h),
                  grp(1, CHUNK_LANES), grp(rows_p, CHUNK_LANES), grp(rows_s, CHUNK_LANES)],
        out_specs=[grp(rows_p, CHUNK_LANES), grp(rows_s, CHUNK_LANES)],
        out_shape=[jax.ShapeDtypeStruct((g, rows_p, CHUNK_LANES), _F32),
                   jax.ShapeDtypeStruct((g, rows_s, CHUNK_LANES), _F32)],
        scratch_shapes=[
            pltpu.VMEM((CHUNK_LANES, CHUNK_LANES), _BF16),
            pltpu.VMEM((CHUNK_LANES, 2 * STATE_LANES), _BF16),
            pltpu.VMEM((2 * STATE_LANES, CHUNK_LANES), _BF16),
            pltpu.VMEM((h, 1, 2 * CHUNK_LANES), _F32),
            pltpu.VMEM((4, h, 1, p), _F32),
            pltpu.VMEM((rows_p, 2 * STATE_LANES), _BF16),
            pltpu.VMEM((rows_s, 2 * STATE_LANES), _BF16),
        ],
        compiler_params=pltpu.CompilerParams(vmem_limit_bytes=VMEM_LIMIT),
        name="ssm_chunked",
    )(arow, acol, bt, ct, dfl, up, us)


def _tail_kernel(x_ref, yf_ref, gy_ref, mod_ref, wglu_ref, bglu_ref, gf_ref, gss_ref,
                 wout_ref, gmlp_ref, w1_ref, b1_ref, w2_ref, b2_ref, gfin_ref, o_ref, gs_scr):
    sub = pl.program_id(1) % TAIL_GROUP
    nck = gy_ref.shape[1]

    @pl.when(sub == 0)
    def _():
        def per_group(g, carry):
            for hh in range(SSM_GROUP):
                gs_scr[pl.ds(g * SSM_GROUP + hh, nck, stride=ZS_PITCH), :] = \
                    gy_ref[g, :, hh * SSM_CHUNK:(hh + 1) * SSM_CHUNK]
            return carry
        lax.fori_loop(0, N_GROUPS, per_group, 0)

    per_tile = nck // TAIL_GROUP
    slabs = []
    for c in range(per_tile):
        r0 = pl.multiple_of((sub * per_tile + c) * ZS_PITCH, SUBLANES)
        slabs.append(gs_scr[pl.ds(r0, D_SSM), :].T)
    gs = jnp.concatenate(slabs, axis=0).astype(_BF16)

    gate1 = mod_ref[2:3, :]
    shift2 = mod_ref[3:4, :]
    scale2 = mod_ref[4:5, :]
    gate2 = mod_ref[5:6, :]

    ab = jnp.dot(gs, wglu_ref[...], preferred_element_type=_F32) + bglu_ref[...]
    ys = ab[:, :D_SSM] * jax.nn.sigmoid(ab[:, D_SSM:])
    ysn = _rms(ys) * gss_ref[...]
    yfn = _rms(yf_ref[...].astype(_F32)) * gf_ref[...]
    merged = jnp.concatenate([yfn, ysn], axis=-1).astype(_BF16)
    x1 = x_ref[...] + gate1 * jnp.dot(merged, wout_ref[...], preferred_element_type=_F32)

    h2 = (_rms(x1) * gmlp_ref[...] * (1.0 + scale2) + shift2).astype(_BF16)
    ff = jnp.zeros(x1.shape, _F32)
    for c0 in range(0, D_FF, FF_CHUNK):
        a = jnp.dot(h2, w1_ref[:, c0:c0 + FF_CHUNK], preferred_element_type=_F32)
        a = jnp.maximum(a + b1_ref[:, c0:c0 + FF_CHUNK], 0.0)
        a = (a * a).astype(_BF16)
        ff = ff + jnp.dot(a, w2_ref[c0:c0 + FF_CHUNK, :], preferred_element_type=_F32)
    x2 = x1 + gate2 * (ff + b2_ref[...])
    o_ref[...] = _rms(x2) * gfin_ref[...]


def _tail(x, yf, gy, mod, wglu, b_glu, g_f, g_s, wout, g_mlp, w1, b1, w2, b2, g_final):
    b, l, _ = x.shape
    tm = TOKEN_TILE
    nck = TAIL_GROUP * tm // SSM_CHUNK
    blocks = l // (TAIL_GROUP * tm)
    tok = lambda width: pl.BlockSpec((None, tm, width), lambda i, j: (i, j, 0))
    row = lambda a: a.reshape(1, -1)
    single = lambda shape: pl.BlockSpec(shape, lambda *_: (0,) * len(shape),
                                        pipeline_mode=pl.Buffered(1))
    return pl.pallas_call(
        _tail_kernel,
        grid=(b, l // tm),
        in_specs=[
            tok(D_MODEL), tok(D_FOURIER),
            pl.BlockSpec((N_GROUPS, nck, CHUNK_LANES),
                         lambda i, j: (0, i * blocks + j // TAIL_GROUP, 0)),
            pl.BlockSpec((None, N_MOD, D_MODEL), lambda i, j: (i, 0, 0)),
            single((D_SSM, 2 * D_SSM)), single((1, 2 * D_SSM)),
            single((1, D_FOURIER)), single((1, D_SSM)),
            single((D_MODEL, D_MODEL)), single((1, D_MODEL)),
            single((D_MODEL, D_FF)), single((1, D_FF)),
            single((D_FF, D_MODEL)), single((1, D_MODEL)), single((1, D_MODEL)),
        ],
        out_specs=tok(D_MODEL),
        out_shape=jax.ShapeDtypeStruct((b, l, D_MODEL), _F32),
        scratch_shapes=[pltpu.VMEM((nck * ZS_PITCH, LANES), _F32)],
        compiler_params=pltpu.CompilerParams(
            dimension_semantics=("arbitrary", "arbitrary"), vmem_limit_bytes=VMEM_LIMIT),
        name="tail",
    )(x, yf, gy, mod, wglu, row(b_glu), row(g_f), row(g_s), wout, row(g_mlp),
      w1, row(b1), w2, row(b2), row(g_final))


def kernel(x_prompt, x_sample, c_prompt, c_sample, w_ada, b_ada, g_mix_norm, w_in, w_fourier, b_fourier, ssm_a_re, ssm_a_im, ssm_log_dt, ssm_b_re, ssm_b_im, ssm_c_re, ssm_c_im, ssm_d, w_glu, b_glu, g_fourier_out, g_ssm_out, w_out, g_mlp_norm, w_mlp_in, b_mlp_in, w_mlp_out, b_mlp_out, g_final):
    assert w_ada.shape[0] == 1, "single-layer block"
    n_p = c_prompt.shape[0]
    n_s = c_sample.shape[0]
    pad = (-(n_p + n_s)) % 8
    c_all = jnp.concatenate(
        [c_prompt, c_sample, jnp.zeros((pad, D_MODEL), c_prompt.dtype)], axis=0)
    mod = _adaln(c_all, w_ada[0], b_ada[0]).reshape(-1, N_MOD, D_MODEL)

    wf_b = w_in[0][:, :D_FOURIER].astype(_BF16)
    wst_b = w_in[0][:, D_FOURIER:].T.astype(_BF16)
    wcomb = _fold_fourier_weight(w_fourier[0])
    wglu_b = w_glu[0].astype(_BF16)
    wout_b = w_out[0].astype(_BF16)
    w1_b = w_mlp_in[0].astype(_BF16)
    w2_b = w_mlp_out[0].astype(_BF16)
    mod_p = mod[:n_p]
    mod_s = mod[n_p:n_p + n_s]

    x2d_p, u_p = _inproj(x_prompt, mod_p, g_mix_norm[0], wf_b, wst_b)
    x2d_s, u_s = _inproj(x_sample, mod_s, g_mix_norm[0], wf_b, wst_b)
    gy_p, gy_s = _ssm_mixer(u_p, u_s, x_prompt.shape[1] // SSM_CHUNK,
                            x_sample.shape[1] // SSM_CHUNK, ssm_a_re[0], ssm_a_im[0],
                            ssm_log_dt[0], ssm_b_re[0], ssm_b_im[0], ssm_c_re[0],
                            ssm_c_im[0], ssm_d[0])

    def finish(x, m, x2d, gy):
        yf = _fourier_mixer(x2d, wcomb, b_fourier[0])
        return _tail(x, yf, gy, m, wglu_b, b_glu[0], g_fourier_out[0], g_ssm_out[0],
                     wout_b, g_mlp_norm[0], w1_b, b_mlp_in[0], w2_b, b_mlp_out[0], g_final)

    return (finish(x_prompt, mod_p, x2d_p, gy_p), finish(x_sample, mod_s, x2d_s, gy_s))
```

```python
import functools
import math

import numpy as np
import jax
import jax.numpy as jnp
from jax import lax
from jax.experimental import pallas as pl
from jax.experimental.pallas import tpu as pltpu

D_MODEL = 1024
D_FOURIER = 512
D_SSM = 512
N_HEADS = 4
HEAD_DIM = 128
SSM_GROUP = 16
N_GROUPS = 32
SSM_STATE = 64
D_FF = 4096
N_MOD = 6
EPS = 1e-6
A_RE_MAX = -1e-4

SSM_CHUNK = 128
CHUNK_LANES = SSM_CHUNK * SSM_GROUP
STATE_LANES = 2 * SSM_STATE
W_COL_BLOCKS = 4
DFT_N1 = 128
LANES = 128
SUBLANES = 8
ROW_PAD = 8
ZS_PITCH = D_SSM + ROW_PAD
INPROJ_TILE = 1024
TOKEN_TILE = 512
TAIL_GROUP = INPROJ_TILE // TOKEN_TILE
FF_CHUNK = 1024
VMEM_LIMIT = 56 * 1024 * 1024

_BF16 = jnp.bfloat16
_F32 = jnp.float32


def _rms(v):
    return v * lax.rsqrt(jnp.mean(v * v, axis=-1, keepdims=True) + EPS)


def _const_spec(shape):
    nd = len(shape)
    return pl.BlockSpec(shape, lambda *_: (0,) * nd)


def _adaln_kernel(c_ref, w_ref, b_ref, o_ref):
    c = c_ref[...]
    s = c * jax.nn.sigmoid(c)
    o_ref[...] = jnp.dot(s, w_ref[...], preferred_element_type=_F32,
                         precision=lax.Precision.HIGHEST) + b_ref[...]


def _adaln(c_all, w_ada, b_ada):
    rows = c_all.shape[0]
    return pl.pallas_call(
        _adaln_kernel,
        grid=(N_MOD,),
        in_specs=[
            _const_spec((rows, D_MODEL)),
            pl.BlockSpec((D_MODEL, D_MODEL), lambda j: (0, j)),
            pl.BlockSpec((1, D_MODEL), lambda j: (0, j)),
        ],
        out_specs=pl.BlockSpec((rows, D_MODEL), lambda j: (0, j)),
        out_shape=jax.ShapeDtypeStruct((rows, N_MOD * D_MODEL), _F32),
        name="adaln",
    )(c_all, w_ada, b_ada.reshape(1, -1))


def _fold_kernel(cs_ref, wf_ref, o_ref):
    o_ref[...] = jnp.dot(cs_ref[...], wf_ref[...], preferred_element_type=_F32,
                         precision=lax.Precision.HIGHEST).astype(o_ref.dtype)


@functools.lru_cache(maxsize=None)
def _channel_dft_np():
    k = np.arange(D_FOURIER)
    ang = 2.0 * np.pi * ((k[:, None] * k[None, :]) % D_FOURIER) / D_FOURIER
    scale = 1.0 / math.sqrt(D_FOURIER)
    return np.concatenate([np.cos(ang), np.sin(ang)], axis=0).astype(np.float32) * scale


def _fold_fourier_weight(w_fourier):
    wblk = jnp.zeros((D_FOURIER, D_FOURIER), _F32)
    for h in range(N_HEADS):
        sl = slice(h * HEAD_DIM, (h + 1) * HEAD_DIM)
        wblk = wblk.at[sl, sl].set(w_fourier[h])
    cs = jnp.asarray(_channel_dft_np())
    return pl.pallas_call(
        _fold_kernel,
        out_shape=jax.ShapeDtypeStruct((2 * D_FOURIER, D_FOURIER), _BF16),
        name="fold_fourier_weight",
    )(cs, wblk)


def _inproj_kernel(x_ref, mod_ref, g_ref, wf_ref, wst_ref, x2d_ref, u_ref, zf_scr, zs_scr,
                   *, n2):
    x = x_ref[...]
    tm = x.shape[0]
    h = (_rms(x) * g_ref[...] * (1.0 + mod_ref[1:2, :]) + mod_ref[0:1, :]).astype(_BF16)
    zf = jnp.dot(h, wf_ref[...], preferred_element_type=_F32)
    zst = lax.dot_general(wst_ref[...], h, (((1,), (1,)), ((), ())),
                          preferred_element_type=_F32)

    r1 = tm // n2
    pitch = n2 + ROW_PAD
    for q in range(D_FOURIER // LANES):
        for blk in range(r1):
            zf_scr[q, blk * pitch:blk * pitch + n2, :] = \
                zf[blk * n2:(blk + 1) * n2, q * LANES:(q + 1) * LANES]
    for j2 in range(n2):
        for a in range(r1 // SUBLANES):
            for q in range(D_FOURIER // LANES):
                c0 = j2 * D_FOURIER + q * LANES
                x2d_ref[SUBLANES * a:SUBLANES * (a + 1), c0:c0 + LANES] = \
                    zf_scr[q, pl.ds(j2 + SUBLANES * a * pitch, SUBLANES, stride=pitch), :]

    nck = tm // SSM_CHUNK
    for c in range(nck):
        zs_scr[c * ZS_PITCH:c * ZS_PITCH + D_SSM, :] = zst[:, c * SSM_CHUNK:(c + 1) * SSM_CHUNK]

    for g in range(N_GROUPS):
        for hh in range(SSM_GROUP):
            u_ref[g, :, hh * SSM_CHUNK:(hh + 1) * SSM_CHUNK] = \
                zs_scr[pl.ds(g * SSM_GROUP + hh, nck, stride=ZS_PITCH), :]


def _inproj(x, mod, g_mix, wf_bf16, wst_bf16):
    b, l, _ = x.shape
    tm = INPROJ_TILE
    n2 = l // DFT_N1
    r1 = tm // n2
    nck = tm // SSM_CHUNK
    steps = l // tm
    return pl.pallas_call(
        functools.partial(_inproj_kernel, n2=n2),
        grid=(b, steps),
        in_specs=[
            pl.BlockSpec((None, tm, D_MODEL), lambda i, j: (i, j, 0)),
            pl.BlockSpec((None, N_MOD, D_MODEL), lambda i, j: (i, 0, 0)),
            _const_spec((1, D_MODEL)),
            _const_spec((D_MODEL, D_FOURIER)),
            _const_spec((D_SSM, D_MODEL)),
        ],
        out_specs=[
            pl.BlockSpec((None, r1, n2 * D_FOURIER), lambda i, j: (i, j, 0)),
            pl.BlockSpec((N_GROUPS, nck, CHUNK_LANES), lambda i, j: (0, i * steps + j, 0)),
        ],
        out_shape=[
            jax.ShapeDtypeStruct((b, DFT_N1, n2 * D_FOURIER), _F32),
            jax.ShapeDtypeStruct((N_GROUPS, b * l // SSM_CHUNK, CHUNK_LANES), _F32),
        ],
        scratch_shapes=[
            pltpu.VMEM((D_FOURIER // LANES, r1 * (n2 + ROW_PAD), LANES), _F32),
            pltpu.VMEM((nck * ZS_PITCH, LANES), _F32),
        ],
        compiler_params=pltpu.CompilerParams(vmem_limit_bytes=VMEM_LIMIT),
        name="inproj",
    )(x, mod, g_mix.reshape(1, -1), wf_bf16, wst_bf16)


@functools.lru_cache(maxsize=None)
def _dft_tables_np(n):
    n1 = DFT_N1
    n2 = n // n1
    i1 = np.arange(n1)
    ang1 = 2.0 * np.pi * ((i1[:, None] * i1[None, :]) % n1) / n1
    f1 = np.concatenate([np.cos(ang1), -np.sin(ang1)], axis=0).astype(np.float32)
    i2 = np.arange(n2)
    m = (i2[None, None, :] * (i1[:, None, None] + n1 * i2[None, :, None])) % n
    ang = 2.0 * np.pi * m / n
    gr = np.cos(ang) / math.sqrt(n)
    gi = -np.sin(ang) / math.sqrt(n)
    g = np.concatenate([np.concatenate([gr, -gi], axis=2),
                        np.concatenate([gi, gr], axis=2)], axis=1).astype(np.float32)
    return f1, g


def _dft1_kernel(f_ref, x_ref, a_ref):
    a_ref[...] = jnp.dot(f_ref[...], x_ref[...].astype(_BF16),
                         preferred_element_type=_F32).astype(_BF16)


def _dft_stage1(x2d, f1):
    b, n1, cols = x2d.shape
    cw = 4096
    return pl.pallas_call(
        _dft1_kernel,
        grid=(b, cols // cw),
        in_specs=[
            _const_spec((2 * n1, n1)),
            pl.BlockSpec((None, n1, cw), lambda i, j: (i, 0, j)),
        ],
        out_specs=pl.BlockSpec((None, 2 * n1, cw), lambda i, j: (i, 0, j)),
        out_shape=jax.ShapeDtypeStruct((b, 2 * n1, cols), _BF16),
        name="dft_stage1",
    )(f1, x2d)


def _dft2_kernel(a_ref, g_ref, wc_ref, bf_ref, o_ref, *, kb, n2):
    xr, xi = [], []
    for j in range(kb):
        aa = jnp.concatenate([a_ref[0, j], a_ref[1, j]], axis=0)
        x = jnp.dot(g_ref[j], aa, preferred_element_type=_F32)
        xr.append(x[:n2].astype(_BF16))
        xi.append(x[n2:].astype(_BF16))
    xr = jnp.concatenate(xr, axis=0)
    xi = jnp.concatenate(xi, axis=0)
    y = jnp.dot(xr, wc_ref[:D_FOURIER, :], preferred_element_type=_F32)
    y = y + jnp.dot(xi, wc_ref[D_FOURIER:, :], preferred_element_type=_F32)
    y = y + bf_ref[...]
    o_ref[...] = y.reshape(kb, n2, D_FOURIER).astype(o_ref.dtype)


def _dft_stage2(a, g, wcomb, b_fourier):
    b = a.shape[0]
    n1 = DFT_N1
    n2 = g.shape[1] // 2
    kb = max(8, 1024 // n2)
    a5 = a.reshape(b, 2, n1, n2, D_FOURIER)
    return pl.pallas_call(
        functools.partial(_dft2_kernel, kb=kb, n2=n2),
        grid=(b, n1 // kb),
        in_specs=[
            pl.BlockSpec((None, 2, kb, n2, D_FOURIER), lambda i, j: (i, 0, j, 0, 0)),
            pl.BlockSpec((kb, 2 * n2, 2 * n2), lambda i, j: (j, 0, 0)),
            _const_spec((2 * D_FOURIER, D_FOURIER)),
            _const_spec((1, D_FOURIER)),
        ],
        out_specs=pl.BlockSpec((None, kb, n2, D_FOURIER), lambda i, j: (i, j, 0, 0)),
        out_shape=jax.ShapeDtypeStruct((b, n1, n2, D_FOURIER), _BF16),
        name="dft_stage2",
    )(a5, g, wcomb, b_fourier.reshape(1, -1))


def _fourier_mixer(x2d, wcomb, b_fourier):
    b = x2d.shape[0]
    l = x2d.shape[1] * x2d.shape[2] // D_FOURIER
    f1_np, g_np = _dft_tables_np(l)
    f1 = jnp.asarray(f1_np).astype(_BF16)
    g = jnp.asarray(g_np).astype(_BF16)
    a = _dft_stage1(x2d, f1)
    y = _dft_stage2(a, g, wcomb, b_fourier)
    return jnp.swapaxes(y, 1, 2).reshape(b, l, D_FOURIER)


def _cpow(k, zr, zi):
    mag = jnp.exp(k * zr)
    return mag * jnp.cos(k * zi), mag * jnp.sin(k * zi)


def _discretise(a_re, a_im, log_dt):
    lam_re = jnp.minimum(a_re, A_RE_MAX)
    dt = jnp.exp(log_dt)
    return lam_re, a_im, lam_re * dt, a_im * dt


def _ssm_kernel(arow_ref, acol_ref, bt_ref, ct_ref, d_ref, up_ref, us_ref,
                op_ref, os_ref, w_scr, bm_scr, e_scr, ub_scr, hc_scr, *, chunks_p, chunks_s):
    t = SSM_CHUNK
    p = SSM_STATE
    hi = lax.Precision.HIGHEST
    lane_k = lax.broadcasted_iota(jnp.int32, (p, t), 1).astype(_F32)
    sub_k = lax.broadcasted_iota(jnp.int32, (t, p), 0).astype(_F32)
    lane0 = lax.broadcasted_iota(jnp.int32, (p, t), 1) == 0

    zrow, bb = [], []
    for d in range(2):
        lam_re, lam_im, zr, zi = _discretise(arow_ref[d, 0:1, :], arow_ref[d, 1:2, :],
                                             arow_ref[d, 2:3, :])
        th = jnp.tanh(0.5 * zr)
        em1 = 2.0 * th / (1.0 - th)
        e_re = em1 * jnp.cos(zi) - 2.0 * jnp.square(jnp.sin(0.5 * zi))
        e_im = (em1 + 1.0) * jnp.sin(zi)
        den = lam_re * lam_re + lam_im * lam_im
        q_re = (e_re * lam_re + e_im * lam_im) / den
        q_im = (e_im * lam_re - e_re * lam_im) / den
        b_re = bt_ref[d, 0]
        b_im = bt_ref[d, 1]
        bb.append((q_re * b_re - q_im * b_im, q_re * b_im + q_im * b_re))
        zrow.append((zr, zi))

    def c_times(d, pw_re, pw_im):
        re, im = [], []
        for h in range(SSM_GROUP):
            c_re = jnp.broadcast_to(ct_ref[d, 0][:, h:h + 1], (p, t))
            c_im = jnp.broadcast_to(ct_ref[d, 1][:, h:h + 1], (p, t))
            re.append(c_re * pw_re - c_im * pw_im)
            im.append(c_re * pw_im + c_im * pw_re)
        return jnp.concatenate(re, axis=1), jnp.concatenate(im, axis=1)

    _, _, zr0, zi0 = _discretise(acol_ref[0, 0], acol_ref[0, 1], acol_ref[0, 2])
    _, _, zr1, zi1 = _discretise(acol_ref[1, 0], acol_ref[1, 1], acol_ref[1, 2])
    caf_re, caf_im = c_times(0, *_cpow(lane_k, zr0, zi0))
    ca1_re, ca1_im = c_times(0, *_cpow(lane_k + 1.0, zr0, zi0))
    cab_re, cab_im = c_times(1, *_cpow(float(t) - lane_k, zr1, zi1))
    one = jnp.where(lane0, 1.0, 0.0)
    cb0_re, cb0_im = c_times(1, one, jnp.zeros_like(one))

    e_scr[0 * p:1 * p, :] = ca1_re.astype(_BF16)
    e_scr[1 * p:2 * p, :] = (-ca1_im).astype(_BF16)
    e_scr[2 * p:3 * p, :] = cab_re.astype(_BF16)
    e_scr[3 * p:4 * p, :] = (-cab_im).astype(_BF16)

    (bbf_re, bbf_im), (bbb_re, bbb_im) = bb
    lhs_f = jnp.concatenate([bbf_re, -bbf_im, bbb_re, -bbb_im], axis=1)
    rhs_f = jnp.concatenate([caf_re, caf_im, cb0_re, cb0_im], axis=0)
    kf = jnp.dot(lhs_f, rhs_f, preferred_element_type=_F32, precision=hi)
    lhs_b = jnp.concatenate([bbb_re, -bbb_im], axis=1)
    rhs_b = jnp.concatenate([cab_re, cab_im], axis=0)
    kb = jnp.dot(lhs_b, rhs_b, preferred_element_type=_F32, precision=hi)

    pf_re, pf_im = _cpow(float(t - 1) - sub_k, *zrow[0])
    pb_re, pb_im = _cpow(sub_k, *zrow[1])
    for hp in range(SSM_GROUP):
        f_re, f_im = bbf_re[hp:hp + 1, :], bbf_im[hp:hp + 1, :]
        g_re, g_im = bbb_re[hp:hp + 1, :], bbb_im[hp:hp + 1, :]
        blk = jnp.concatenate([pf_re * f_re - pf_im * f_im, pf_re * f_im + pf_im * f_re,
                               pb_re * g_re - pb_im * g_im, pb_re * g_im + pb_im * g_re],
                              axis=1)
        bm_scr[hp * t:(hp + 1) * t, :] = blk.astype(_BF16)

    sl = STATE_LANES
    rows_p = up_ref.shape[0]
    ub_scr[:rows_p, :] = up_ref[...].astype(_BF16)
    ub_scr[rows_p:, :] = us_ref[...].astype(_BF16)
    s_all = jnp.dot(ub_scr[...], bm_scr[...], preferred_element_type=_F32)

    def carry_scan(s, n_chunks, r0):
        rows = s.shape[0]
        pos = lax.broadcasted_iota(jnp.int32, (rows, sl), 0) & (n_chunks - 1)

        def shift_rows(x, n, down):
            if down:
                return jnp.where(pos >= n, pltpu.roll(x, n, 0), 0.0)
            return jnp.where(pos < n_chunks - n, pltpu.roll(x, rows - n, 0), 0.0)

        def cmul(x, ar, ai):
            a1 = jnp.concatenate([ar, ar], axis=1)
            a2 = jnp.concatenate([-ai, ai], axis=1)
            return x * a1 + pltpu.roll(x, SSM_STATE, 1) * a2

        xf = s[:, :sl]
        xb = s[:, sl:]
        dist = 1
        while dist < n_chunks:
            kk = float(t * dist)
            xf = xf + cmul(shift_rows(xf, dist, True), *_cpow(kk, *zrow[0]))
            xb = xb + cmul(shift_rows(xb, dist, False), *_cpow(kk, *zrow[1]))
            dist *= 2
        hc_scr[r0:r0 + rows, :sl] = shift_rows(xf, 1, True).astype(_BF16)
        hc_scr[r0:r0 + rows, sl:] = shift_rows(xb, 1, False).astype(_BF16)

    carry_scan(s_all[:rows_p], chunks_p, 0)
    carry_scan(s_all[rows_p:], chunks_s, rows_p)

    hb = SSM_GROUP // W_COL_BLOCKS
    for nb in range(W_COL_BLOCKS):
        slot = nb % 2
        for hp in range(SSM_GROUP):
            for hl in range(hb):
                h = nb * hb + hl
                lag = jnp.concatenate([kf[hp:hp + 1, h * t:(h + 1) * t],
                                       kb[hp:hp + 1, h * t:(h + 1) * t]], axis=1)
                skew = pltpu.roll(jnp.broadcast_to(lag, (t, 2 * t)), 0, 1,
                                  stride=1, stride_axis=0)
                w_scr[slot, hp * t:(hp + 1) * t, hl * t:(hl + 1) * t] = \
                    skew[:, :t].astype(_BF16)
        c0, c1 = nb * hb * t, (nb + 1) * hb * t
        y = jnp.dot(ub_scr[...], w_scr[slot], preferred_element_type=_F32)
        y = y + jnp.dot(hc_scr[...], e_scr[:, c0:c1], preferred_element_type=_F32)
        yp = y[:rows_p] + d_ref[:, c0:c1] * up_ref[:, c0:c1]
        ys = y[rows_p:] + d_ref[:, c0:c1] * us_ref[:, c0:c1]
        op_ref[:, c0:c1] = jax.nn.gelu(yp, approximate=True)
        os_ref[:, c0:c1] = jax.nn.gelu(ys, approximate=True)


def _ssm_mixer(up, us, chunks_p, chunks_s, a_re, a_im, log_dt, b_re, b_im, c_re, c_im, d_skip):
    g, p, h, t = N_GROUPS, SSM_STATE, SSM_GROUP, SSM_CHUNK
    ldt = jnp.broadcast_to(log_dt[:, :, None], (2, g, p))
    arow = jnp.stack([a_re, a_im, ldt] + [jnp.zeros_like(a_re)] * 5, axis=2)
    arow = jnp.transpose(arow, (1, 0, 2, 3)).astype(_F32)
    acol = jnp.stack([a_re, a_im, ldt], axis=2)
    acol = jnp.transpose(acol, (1, 0, 2, 3)).astype(_F32)
    acol = jnp.broadcast_to(acol[..., None], (g, 2, 3, p, t))
    bt = jnp.transpose(jnp.stack([b_re, b_im], axis=2), (1, 0, 2, 4, 3)).astype(_F32)
    ct = jnp.transpose(jnp.stack([c_re, c_im], axis=2), (1, 0, 2, 4, 3)).astype(_F32)
    dfl = jnp.repeat(d_skip.astype(_F32).reshape(g, 1, h), t, axis=2)

    rows_p, rows_s = up.shape[1], us.shape[1]
    grp = lambda *tail: pl.BlockSpec((None,) + tail, lambda i: (i,) + (0,) * len(tail))
    return pl.pallas_call(
        functools.partial(_ssm_kernel, chunks_p=chunks_p, chunks_s=chunks_s),
        grid=(g,),
        in_specs=[grp(2, 8, p), grp(2, 3, p, t), grp(2, 2, h, p), grp(2, 2, p, h),
                  grp(1, CHUNK_LANES), grp(rows_p, CHUNK_LANES), grp(rows_s, CHUNK_LANES)],
        out_specs=[grp(rows_p, CHUNK_LANES), grp(rows_s, CHUNK_LANES)],
        out_shape=[jax.ShapeDtypeStruct((g, rows_p, CHUNK_LANES), _F32),
                   jax.ShapeDtypeStruct((g, rows_s, CHUNK_LANES), _F32)],
        scratch_shapes=[
            pltpu.VMEM((2, CHUNK_LANES, CHUNK_LANES // W_COL_BLOCKS), _BF16),
            pltpu.VMEM((CHUNK_LANES, 2 * STATE_LANES), _BF16),
            pltpu.VMEM((2 * STATE_LANES, CHUNK_LANES), _BF16),
            pltpu.VMEM((rows_p + rows_s, CHUNK_LANES), _BF16),
            pltpu.VMEM((rows_p + rows_s, 2 * STATE_LANES), _BF16),
        ],
        compiler_params=pltpu.CompilerParams(vmem_limit_bytes=VMEM_LIMIT),
        name="ssm_chunked",
    )(arow, acol, bt, ct, dfl, up, us)


def _tail_kernel(x_ref, yf_ref, gy_ref, mod_ref, wglu_ref, bglu_ref, gf_ref, gss_ref,
                 wout_ref, gmlp_ref, w1_ref, b1_ref, w2_ref, b2_ref, gfin_ref, o_ref, gs_scr):
    sub = pl.program_id(1) % TAIL_GROUP
    nck = gy_ref.shape[1]

    @pl.when(sub == 0)
    def _():
        def per_group(g, carry):
            for hh in range(SSM_GROUP):
                gs_scr[pl.ds(g * SSM_GROUP + hh, nck, stride=ZS_PITCH), :] = \
                    gy_ref[g, :, hh * SSM_CHUNK:(hh + 1) * SSM_CHUNK]
            return carry
        lax.fori_loop(0, N_GROUPS, per_group, 0)

    per_tile = nck // TAIL_GROUP
    slabs = []
    for c in range(per_tile):
        r0 = pl.multiple_of((sub * per_tile + c) * ZS_PITCH, SUBLANES)
        slabs.append(gs_scr[pl.ds(r0, D_SSM), :].T)
    gs = jnp.concatenate(slabs, axis=0).astype(_BF16)

    gate1 = mod_ref[2:3, :]
    shift2 = mod_ref[3:4, :]
    scale2 = mod_ref[4:5, :]
    gate2 = mod_ref[5:6, :]

    ab = jnp.dot(gs, wglu_ref[...], preferred_element_type=_F32) + bglu_ref[...]
    ys = ab[:, :D_SSM] * jax.nn.sigmoid(ab[:, D_SSM:])
    ysn = _rms(ys) * gss_ref[...]
    yfn = _rms(yf_ref[...].astype(_F32)) * gf_ref[...]
    merged = jnp.concatenate([yfn, ysn], axis=-1).astype(_BF16)
    x1 = x_ref[...] + gate1 * jnp.dot(merged, wout_ref[...], preferred_element_type=_F32)

    h2 = (_rms(x1) * gmlp_ref[...] * (1.0 + scale2) + shift2).astype(_BF16)
    ff = jnp.zeros(x1.shape, _F32)
    for c0 in range(0, D_FF, FF_CHUNK):
        a = jnp.dot(h2, w1_ref[:, c0:c0 + FF_CHUNK], preferred_element_type=_F32)
        a = jnp.maximum(a + b1_ref[:, c0:c0 + FF_CHUNK], 0.0)
        a = (a * a).astype(_BF16)
        ff = ff + jnp.dot(a, w2_ref[c0:c0 + FF_CHUNK, :], preferred_element_type=_F32)
    x2 = x1 + gate2 * (ff + b2_ref[...])
    o_ref[...] = _rms(x2) * gfin_ref[...]


def _tail(x, yf, gy, mod, wglu, b_glu, g_f, g_s, wout, g_mlp, w1, b1, w2, b2, g_final):
    b, l, _ = x.shape
    tm = TOKEN_TILE
    nck = TAIL_GROUP * tm // SSM_CHUNK
    blocks = l // (TAIL_GROUP * tm)
    tok = lambda width: pl.BlockSpec((None, tm, width), lambda i, j: (i, j, 0))
    row = lambda a: a.reshape(1, -1)
    single = lambda shape: pl.BlockSpec(shape, lambda *_: (0,) * len(shape),
                                        pipeline_mode=pl.Buffered(1))
    return pl.pallas_call(
        _tail_kernel,
        grid=(b, l // tm),
        in_specs=[
            tok(D_MODEL), tok(D_FOURIER),
            pl.BlockSpec((N_GROUPS, nck, CHUNK_LANES),
                         lambda i, j: (0, i * blocks + j // TAIL_GROUP, 0)),
            pl.BlockSpec((None, N_MOD, D_MODEL), lambda i, j: (i, 0, 0)),
            single((D_SSM, 2 * D_SSM)), single((1, 2 * D_SSM)),
            single((1, D_FOURIER)), single((1, D_SSM)),
            single((D_MODEL, D_MODEL)), single((1, D_MODEL)),
            single((D_MODEL, D_FF)), single((1, D_FF)),
            single((D_FF, D_MODEL)), single((1, D_MODEL)), single((1, D_MODEL)),
        ],
        out_specs=tok(D_MODEL),
        out_shape=jax.ShapeDtypeStruct((b, l, D_MODEL), _F32),
        scratch_shapes=[pltpu.VMEM((nck * ZS_PITCH, LANES), _F32)],
        compiler_params=pltpu.CompilerParams(
            dimension_semantics=("arbitrary", "arbitrary"), vmem_limit_bytes=VMEM_LIMIT),
        name="tail",
    )(x, yf, gy, mod, wglu, row(b_glu), row(g_f), row(g_s), wout, row(g_mlp),
      w1, row(b1), w2, row(b2), row(g_final))


def kernel(x_prompt, x_sample, c_prompt, c_sample, w_ada, b_ada, g_mix_norm, w_in, w_fourier, b_fourier, ssm_a_re, ssm_a_im, ssm_log_dt, ssm_b_re, ssm_b_im, ssm_c_re, ssm_c_im, ssm_d, w_glu, b_glu, g_fourier_out, g_ssm_out, w_out, g_mlp_norm, w_mlp_in, b_mlp_in, w_mlp_out, b_mlp_out, g_final):
    assert w_ada.shape[0] == 1, "single-layer block"
    n_p = c_prompt.shape[0]
    n_s = c_sample.shape[0]
    pad = (-(n_p + n_s)) % 8
    c_all = jnp.concatenate(
        [c_prompt, c_sample, jnp.zeros((pad, D_MODEL), c_prompt.dtype)], axis=0)
    mod = _adaln(c_all, w_ada[0], b_ada[0]).reshape(-1, N_MOD, D_MODEL)

    wf_b = w_in[0][:, :D_FOURIER].astype(_BF16)
    wst_b = w_in[0][:, D_FOURIER:].T.astype(_BF16)
    wcomb = _fold_fourier_weight(w_fourier[0])
    wglu_b = w_glu[0].astype(_BF16)
    wout_b = w_out[0].astype(_BF16)
    w1_b = w_mlp_in[0].astype(_BF16)
    w2_b = w_mlp_out[0].astype(_BF16)
    mod_p = mod[:n_p]
    mod_s = mod[n_p:n_p + n_s]

    x2d_p, u_p = _inproj(x_prompt, mod_p, g_mix_norm[0], wf_b, wst_b)
    x2d_s, u_s = _inproj(x_sample, mod_s, g_mix_norm[0], wf_b, wst_b)
    gy_p, gy_s = _ssm_mixer(u_p, u_s, x_prompt.shape[1] // SSM_CHUNK,
                            x_sample.shape[1] // SSM_CHUNK, ssm_a_re[0], ssm_a_im[0],
                            ssm_log_dt[0], ssm_b_re[0], ssm_b_im[0], ssm_c_re[0],
                            ssm_c_im[0], ssm_d[0])

    def finish(x, m, x2d, gy):
        yf = _fourier_mixer(x2d, wcomb, b_fourier[0])
        return _tail(x, yf, gy, m, wglu_b, b_glu[0], g_fourier_out[0], g_ssm_out[0],
                     wout_b, g_mlp_norm[0], w1_b, b_mlp_in[0], w2_b, b_mlp_out[0], g_final)

    return (finish(x_prompt, mod_p, x2d_p, gy_p), finish(x_sample, mod_s, x2d_s, gy_s))
```

```python
import functools
import math

import numpy as np
import jax
import jax.numpy as jnp
from jax import lax
from jax.experimental import pallas as pl
from jax.experimental.pallas import tpu as pltpu

D_MODEL = 1024
D_FOURIER = 512
D_SSM = 512
N_HEADS = 4
HEAD_DIM = 128
SSM_GROUP = 16
N_GROUPS = 32
SSM_STATE = 64
D_FF = 4096
N_MOD = 6
EPS = 1e-6
A_RE_MAX = -1e-4

SSM_CHUNK = 128
CHUNK_LANES = SSM_CHUNK * SSM_GROUP
STATE_LANES = 2 * SSM_STATE
W_COL_BLOCKS = 4
DFT_N1 = 128
LANES = 128
SUBLANES = 8
ROW_PAD = 8
ZS_PITCH = D_SSM + ROW_PAD
INPROJ_TILE = 1024
TOKEN_TILE = 512
TAIL_GROUP = INPROJ_TILE // TOKEN_TILE
FF_CHUNK = 1024
VMEM_LIMIT = 56 * 1024 * 1024

_BF16 = jnp.bfloat16
_F32 = jnp.float32


def _rms(v):
    return v * lax.rsqrt(jnp.mean(v * v, axis=-1, keepdims=True) + EPS)


def _const_spec(shape):
    nd = len(shape)
    return pl.BlockSpec(shape, lambda *_: (0,) * nd)


def _adaln_kernel(c_ref, w_ref, b_ref, o_ref):
    c = c_ref[...]
    s = c * jax.nn.sigmoid(c)
    o_ref[...] = jnp.dot(s, w_ref[...], preferred_element_type=_F32,
                         precision=lax.Precision.HIGHEST) + b_ref[...]


def _adaln(c_all, w_ada, b_ada):
    rows = c_all.shape[0]
    return pl.pallas_call(
        _adaln_kernel,
        grid=(N_MOD,),
        in_specs=[
            _const_spec((rows, D_MODEL)),
            pl.BlockSpec((D_MODEL, D_MODEL), lambda j: (0, j)),
            pl.BlockSpec((1, D_MODEL), lambda j: (0, j)),
        ],
        out_specs=pl.BlockSpec((rows, D_MODEL), lambda j: (0, j)),
        out_shape=jax.ShapeDtypeStruct((rows, N_MOD * D_MODEL), _F32),
        name="adaln",
    )(c_all, w_ada, b_ada.reshape(1, -1))


def _fold_kernel(cs_ref, wf_ref, o_ref):
    o_ref[...] = jnp.dot(cs_ref[...], wf_ref[...], preferred_element_type=_F32,
                         precision=lax.Precision.HIGHEST).astype(o_ref.dtype)


@functools.lru_cache(maxsize=None)
def _channel_dft_np():
    k = np.arange(D_FOURIER)
    ang = 2.0 * np.pi * ((k[:, None] * k[None, :]) % D_FOURIER) / D_FOURIER
    scale = 1.0 / math.sqrt(D_FOURIER)
    return np.concatenate([np.cos(ang), np.sin(ang)], axis=0).astype(np.float32) * scale


def _fold_fourier_weight(w_fourier):
    wblk = jnp.zeros((D_FOURIER, D_FOURIER), _F32)
    for h in range(N_HEADS):
        sl = slice(h * HEAD_DIM, (h + 1) * HEAD_DIM)
        wblk = wblk.at[sl, sl].set(w_fourier[h])
    cs = jnp.asarray(_channel_dft_np())
    return pl.pallas_call(
        _fold_kernel,
        out_shape=jax.ShapeDtypeStruct((2 * D_FOURIER, D_FOURIER), _BF16),
        name="fold_fourier_weight",
    )(cs, wblk)


def _inproj_kernel(x_ref, mod_ref, g_ref, wf_ref, wst_ref, x2d_ref, u_ref, zf_scr, zs_scr,
                   *, n2):
    x = x_ref[...]
    tm = x.shape[0]
    h = (_rms(x) * g_ref[...] * (1.0 + mod_ref[1:2, :]) + mod_ref[0:1, :]).astype(_BF16)
    zf = jnp.dot(h, wf_ref[...], preferred_element_type=_F32)
    zst = lax.dot_general(wst_ref[...], h, (((1,), (1,)), ((), ())),
                          preferred_element_type=_F32)

    r1 = tm // n2
    pitch = n2 + ROW_PAD
    for q in range(D_FOURIER // LANES):
        for blk in range(r1):
            zf_scr[q, blk * pitch:blk * pitch + n2, :] = \
                zf[blk * n2:(blk + 1) * n2, q * LANES:(q + 1) * LANES]
    for j2 in range(n2):
        for a in range(r1 // SUBLANES):
            for q in range(D_FOURIER // LANES):
                c0 = j2 * D_FOURIER + q * LANES
                x2d_ref[SUBLANES * a:SUBLANES * (a + 1), c0:c0 + LANES] = \
                    zf_scr[q, pl.ds(j2 + SUBLANES * a * pitch, SUBLANES, stride=pitch), :]

    nck = tm // SSM_CHUNK
    for c in range(nck):
        zs_scr[c * ZS_PITCH:c * ZS_PITCH + D_SSM, :] = zst[:, c * SSM_CHUNK:(c + 1) * SSM_CHUNK]

    for g in range(N_GROUPS):
        for hh in range(SSM_GROUP):
            u_ref[g, :, hh * SSM_CHUNK:(hh + 1) * SSM_CHUNK] = \
                zs_scr[pl.ds(g * SSM_GROUP + hh, nck, stride=ZS_PITCH), :]


def _inproj(x, mod, g_mix, wf_bf16, wst_bf16):
    b, l, _ = x.shape
    tm = INPROJ_TILE
    n2 = l // DFT_N1
    r1 = tm // n2
    nck = tm // SSM_CHUNK
    steps = l // tm
    return pl.pallas_call(
        functools.partial(_inproj_kernel, n2=n2),
        grid=(b, steps),
        in_specs=[
            pl.BlockSpec((None, tm, D_MODEL), lambda i, j: (i, j, 0)),
            pl.BlockSpec((None, N_MOD, D_MODEL), lambda i, j: (i, 0, 0)),
            _const_spec((1, D_MODEL)),
            _const_spec((D_MODEL, D_FOURIER)),
            _const_spec((D_SSM, D_MODEL)),
        ],
        out_specs=[
            pl.BlockSpec((None, r1, n2 * D_FOURIER), lambda i, j: (i, j, 0)),
            pl.BlockSpec((N_GROUPS, nck, CHUNK_LANES), lambda i, j: (0, i * steps + j, 0)),
        ],
        out_shape=[
            jax.ShapeDtypeStruct((b, DFT_N1, n2 * D_FOURIER), _F32),
            jax.ShapeDtypeStruct((N_GROUPS, b * l // SSM_CHUNK, CHUNK_LANES), _F32),
        ],
        scratch_shapes=[
            pltpu.VMEM((D_FOURIER // LANES, r1 * (n2 + ROW_PAD), LANES), _F32),
            pltpu.VMEM((nck * ZS_PITCH, LANES), _F32),
        ],
        compiler_params=pltpu.CompilerParams(vmem_limit_bytes=VMEM_LIMIT),
        name="inproj",
    )(x, mod, g_mix.reshape(1, -1), wf_bf16, wst_bf16)


@functools.lru_cache(maxsize=None)
def _dft_tables_np(n):
    n1 = DFT_N1
    n2 = n // n1
    i1 = np.arange(n1)
    ang1 = 2.0 * np.pi * ((i1[:, None] * i1[None, :]) % n1) / n1
    f1 = np.concatenate([np.cos(ang1), -np.sin(ang1)], axis=0).astype(np.float32)
    i2 = np.arange(n2)
    m = (i2[None, None, :] * (i1[:, None, None] + n1 * i2[None, :, None])) % n
    ang = 2.0 * np.pi * m / n
    gr = np.cos(ang) / math.sqrt(n)
    gi = -np.sin(ang) / math.sqrt(n)
    g = np.concatenate([np.concatenate([gr, -gi], axis=2),
                        np.concatenate([gi, gr], axis=2)], axis=1).astype(np.float32)
    return f1, g


def _dft1_kernel(f_ref, x_ref, a_ref):
    a_ref[...] = jnp.dot(f_ref[...], x_ref[...].astype(_BF16),
                         preferred_element_type=_F32).astype(_BF16)


def _dft_stage1(x2d, f1):
    b, n1, cols = x2d.shape
    cw = 4096
    return pl.pallas_call(
        _dft1_kernel,
        grid=(b, cols // cw),
        in_specs=[
            _const_spec((2 * n1, n1)),
            pl.BlockSpec((None, n1, cw), lambda i, j: (i, 0, j)),
        ],
        out_specs=pl.BlockSpec((None, 2 * n1, cw), lambda i, j: (i, 0, j)),
        out_shape=jax.ShapeDtypeStruct((b, 2 * n1, cols), _BF16),
        name="dft_stage1",
    )(f1, x2d)


def _dft2_kernel(a_ref, g_ref, wc_ref, bf_ref, o_ref, *, kb, n2):
    xr, xi = [], []
    for j in range(kb):
        aa = jnp.concatenate([a_ref[0, j], a_ref[1, j]], axis=0)
        x = jnp.dot(g_ref[j], aa, preferred_element_type=_F32)
        xr.append(x[:n2].astype(_BF16))
        xi.append(x[n2:].astype(_BF16))
    xr = jnp.concatenate(xr, axis=0)
    xi = jnp.concatenate(xi, axis=0)
    y = jnp.dot(xr, wc_ref[:D_FOURIER, :], preferred_element_type=_F32)
    y = y + jnp.dot(xi, wc_ref[D_FOURIER:, :], preferred_element_type=_F32)
    y = y + bf_ref[...]
    o_ref[...] = y.reshape(kb, n2, D_FOURIER).astype(o_ref.dtype)


def _dft_stage2(a, g, wcomb, b_fourier):
    b = a.shape[0]
    n1 = DFT_N1
    n2 = g.shape[1] // 2
    kb = max(8, 1024 // n2)
    a5 = a.reshape(b, 2, n1, n2, D_FOURIER)
    return pl.pallas_call(
        functools.partial(_dft2_kernel, kb=kb, n2=n2),
        grid=(b, n1 // kb),
        in_specs=[
            pl.BlockSpec((None, 2, kb, n2, D_FOURIER), lambda i, j: (i, 0, j, 0, 0)),
            pl.BlockSpec((kb, 2 * n2, 2 * n2), lambda i, j: (j, 0, 0)),
            _const_spec((2 * D_FOURIER, D_FOURIER)),
            _const_spec((1, D_FOURIER)),
        ],
        out_specs=pl.BlockSpec((None, kb, n2, D_FOURIER), lambda i, j: (i, j, 0, 0)),
        out_shape=jax.ShapeDtypeStruct((b, n1, n2, D_FOURIER), _BF16),
        name="dft_stage2",
    )(a5, g, wcomb, b_fourier.reshape(1, -1))


def _fourier_mixer(x2d, wcomb, b_fourier):
    b = x2d.shape[0]
    l = x2d.shape[1] * x2d.shape[2] // D_FOURIER
    f1_np, g_np = _dft_tables_np(l)
    f1 = jnp.asarray(f1_np).astype(_BF16)
    g = jnp.asarray(g_np).astype(_BF16)
    a = _dft_stage1(x2d, f1)
    y = _dft_stage2(a, g, wcomb, b_fourier)
    return jnp.swapaxes(y, 1, 2).reshape(b, l, D_FOURIER)


def _cpow(k, zr, zi):
    mag = jnp.exp(k * zr)
    return mag * jnp.cos(k * zi), mag * jnp.sin(k * zi)


def _discretise(a_re, a_im, log_dt):
    lam_re = jnp.minimum(a_re, A_RE_MAX)
    dt = jnp.exp(log_dt)
    return lam_re, a_im, lam_re * dt, a_im * dt


def _ssm_kernel(arow_ref, acol_ref, bt_ref, ct_ref, d_ref, up_ref, us_ref,
                op_ref, os_ref, w_scr, bm_scr, e_scr, ub_scr, hc_scr, *, chunks_p, chunks_s):
    t = SSM_CHUNK
    p = SSM_STATE
    hi = lax.Precision.HIGHEST
    lane_k = lax.broadcasted_iota(jnp.int32, (p, t), 1).astype(_F32)
    sub_k = lax.broadcasted_iota(jnp.int32, (t, p), 0).astype(_F32)
    lane0 = lax.broadcasted_iota(jnp.int32, (p, t), 1) == 0

    zrow, bb = [], []
    for d in range(2):
        lam_re, lam_im, zr, zi = _discretise(arow_ref[d, 0:1, :], arow_ref[d, 1:2, :],
                                             arow_ref[d, 2:3, :])
        th = jnp.tanh(0.5 * zr)
        em1 = 2.0 * th / (1.0 - th)
        e_re = em1 * jnp.cos(zi) - 2.0 * jnp.square(jnp.sin(0.5 * zi))
        e_im = (em1 + 1.0) * jnp.sin(zi)
        den = lam_re * lam_re + lam_im * lam_im
        q_re = (e_re * lam_re + e_im * lam_im) / den
        q_im = (e_im * lam_re - e_re * lam_im) / den
        b_re = bt_ref[d, 0]
        b_im = bt_ref[d, 1]
        bb.append((q_re * b_re - q_im * b_im, q_re * b_im + q_im * b_re))
        zrow.append((zr, zi))

    def c_times(d, pw_re, pw_im):
        re, im = [], []
        for h in range(SSM_GROUP):
            c_re = jnp.broadcast_to(ct_ref[d, 0][:, h:h + 1], (p, t))
            c_im = jnp.broadcast_to(ct_ref[d, 1][:, h:h + 1], (p, t))
            re.append(c_re * pw_re - c_im * pw_im)
            im.append(c_re * pw_im + c_im * pw_re)
        return jnp.concatenate(re, axis=1), jnp.concatenate(im, axis=1)

    _, _, zr0, zi0 = _discretise(acol_ref[0, 0], acol_ref[0, 1], acol_ref[0, 2])
    _, _, zr1, zi1 = _discretise(acol_ref[1, 0], acol_ref[1, 1], acol_ref[1, 2])
    caf_re, caf_im = c_times(0, *_cpow(lane_k, zr0, zi0))
    ca1_re, ca1_im = c_times(0, *_cpow(lane_k + 1.0, zr0, zi0))
    cab_re, cab_im = c_times(1, *_cpow(float(t) - lane_k, zr1, zi1))
    one = jnp.where(lane0, 1.0, 0.0)
    cb0_re, cb0_im = c_times(1, one, jnp.zeros_like(one))

    e_scr[0 * p:1 * p, :] = ca1_re.astype(_BF16)
    e_scr[1 * p:2 * p, :] = (-ca1_im).astype(_BF16)
    e_scr[2 * p:3 * p, :] = cab_re.astype(_BF16)
    e_scr[3 * p:4 * p, :] = (-cab_im).astype(_BF16)

    (bbf_re, bbf_im), (bbb_re, bbb_im) = bb
    lhs_f = jnp.concatenate([bbf_re, -bbf_im, bbb_re, -bbb_im], axis=1)
    rhs_f = jnp.concatenate([caf_re, caf_im, cb0_re, cb0_im], axis=0)
    kf = jnp.dot(lhs_f, rhs_f, preferred_element_type=_F32, precision=hi)
    lhs_b = jnp.concatenate([bbb_re, -bbb_im], axis=1)
    rhs_b = jnp.concatenate([cab_re, cab_im], axis=0)
    kb = jnp.dot(lhs_b, rhs_b, preferred_element_type=_F32, precision=hi)

    pf_re, pf_im = _cpow(float(t - 1) - sub_k, *zrow[0])
    pb_re, pb_im = _cpow(sub_k, *zrow[1])
    for hp in range(SSM_GROUP):
        f_re, f_im = bbf_re[hp:hp + 1, :], bbf_im[hp:hp + 1, :]
        g_re, g_im = bbb_re[hp:hp + 1, :], bbb_im[hp:hp + 1, :]
        blk = jnp.concatenate([pf_re * f_re - pf_im * f_im, pf_re * f_im + pf_im * f_re,
                               pb_re * g_re - pb_im * g_im, pb_re * g_im + pb_im * g_re],
                              axis=1)
        bm_scr[hp * t:(hp + 1) * t, :] = blk.astype(_BF16)

    sl = STATE_LANES
    rows_p = up_ref.shape[0]
    ub_scr[:rows_p, :] = up_ref[...].astype(_BF16)
    ub_scr[rows_p:, :] = us_ref[...].astype(_BF16)
    s_all = jnp.dot(ub_scr[...], bm_scr[...], preferred_element_type=_F32)

    def carry_scan(s, n_chunks, r0):
        rows = s.shape[0]
        pos = lax.broadcasted_iota(jnp.int32, (rows, sl), 0) & (n_chunks - 1)

        def shift_rows(x, n, down):
            if down:
                return pltpu.roll(x, n, 0) * (pos >= n).astype(_F32)
            return pltpu.roll(x, rows - n, 0) * (pos < n_chunks - n).astype(_F32)

        def cmul(x, ar, ai):
            a1 = jnp.concatenate([ar, ar], axis=1)
            a2 = jnp.concatenate([-ai, ai], axis=1)
            return x * a1 + pltpu.roll(x, SSM_STATE, 1) * a2

        xf = s[:, :sl]
        xb = s[:, sl:]
        dist = 1
        while dist < n_chunks:
            kk = float(t * dist)
            xf = xf + cmul(shift_rows(xf, dist, True), *_cpow(kk, *zrow[0]))
            xb = xb + cmul(shift_rows(xb, dist, False), *_cpow(kk, *zrow[1]))
            dist *= 2
        hc_scr[r0:r0 + rows, :sl] = shift_rows(xf, 1, True).astype(_BF16)
        hc_scr[r0:r0 + rows, sl:] = shift_rows(xb, 1, False).astype(_BF16)

    carry_scan(s_all[:rows_p], chunks_p, 0)
    carry_scan(s_all[rows_p:], chunks_s, rows_p)

    hb = SSM_GROUP // W_COL_BLOCKS
    for nb in range(W_COL_BLOCKS):
        slot = nb % 2
        for hp in range(SSM_GROUP):
            for hl in range(hb):
                h = nb * hb + hl
                lag = jnp.concatenate([kf[hp:hp + 1, h * t:(h + 1) * t],
                                       kb[hp:hp + 1, h * t:(h + 1) * t]], axis=1)
                skew = pltpu.roll(jnp.broadcast_to(lag, (t, 2 * t)), 0, 1,
                                  stride=1, stride_axis=0)
                w_scr[slot, hp * t:(hp + 1) * t, hl * t:(hl + 1) * t] = \
                    skew[:, :t].astype(_BF16)
        c0, c1 = nb * hb * t, (nb + 1) * hb * t
        y = jnp.dot(ub_scr[...], w_scr[slot], preferred_element_type=_F32)
        y = y + jnp.dot(hc_scr[...], e_scr[:, c0:c1], preferred_element_type=_F32)
        yp = y[:rows_p] + d_ref[:, c0:c1] * up_ref[:, c0:c1]
        ys = y[rows_p:] + d_ref[:, c0:c1] * us_ref[:, c0:c1]
        op_ref[:, c0:c1] = jax.nn.gelu(yp, approximate=True)
        os_ref[:, c0:c1] = jax.nn.gelu(ys, approximate=True)


def _ssm_mixer(up, us, chunks_p, chunks_s, a_re, a_im, log_dt, b_re, b_im, c_re, c_im, d_skip):
    g, p, h, t = N_GROUPS, SSM_STATE, SSM_GROUP, SSM_CHUNK
    ldt = jnp.broadcast_to(log_dt[:, :, None], (2, g, p))
    arow = jnp.stack([a_re, a_im, ldt] + [jnp.zeros_like(a_re)] * 5, axis=2)
    arow = jnp.transpose(arow, (1, 0, 2, 3)).astype(_F32)
    acol = jnp.stack([a_re, a_im, ldt], axis=2)
    acol = jnp.transpose(acol, (1, 0, 2, 3)).astype(_F32)
    acol = jnp.broadcast_to(acol[..., None], (g, 2, 3, p, t))
    bt = jnp.transpose(jnp.stack([b_re, b_im], axis=2), (1, 0, 2, 4, 3)).astype(_F32)
    ct = jnp.transpose(jnp.stack([c_re, c_im], axis=2), (1, 0, 2, 4, 3)).astype(_F32)
    dfl = jnp.repeat(d_skip.astype(_F32).reshape(g, 1, h), t, axis=2)

    rows_p, rows_s = up.shape[1], us.shape[1]
    grp = lambda *tail: pl.BlockSpec((None,) + tail, lambda i: (i,) + (0,) * len(tail))
    return pl.pallas_call(
        functools.partial(_ssm_kernel, chunks_p=chunks_p, chunks_s=chunks_s),
        grid=(g,),
        in_specs=[grp(2, 8, p), grp(2, 3, p, t), grp(2, 2, h, p), grp(2, 2, p, h),
                  grp(1, CHUNK_LANES), grp(rows_p, CHUNK_LANES), grp(rows_s, CHUNK_LANES)],
        out_specs=[grp(rows_p, CHUNK_LANES), grp(rows_s, CHUNK_LANES)],
        out_shape=[jax.ShapeDtypeStruct((g, rows_p, CHUNK_LANES), _F32),
                   jax.ShapeDtypeStruct((g, rows_s, CHUNK_LANES), _F32)],
        scratch_shapes=[
            pltpu.VMEM((2, CHUNK_LANES, CHUNK_LANES // W_COL_BLOCKS), _BF16),
            pltpu.VMEM((CHUNK_LANES, 2 * STATE_LANES), _BF16),
            pltpu.VMEM((2 * STATE_LANES, CHUNK_LANES), _BF16),
            pltpu.VMEM((rows_p + rows_s, CHUNK_LANES), _BF16),
            pltpu.VMEM((rows_p + rows_s, 2 * STATE_LANES), _BF16),
        ],
        compiler_params=pltpu.CompilerParams(vmem_limit_bytes=VMEM_LIMIT),
        name="ssm_chunked",
    )(arow, acol, bt, ct, dfl, up, us)


def _tail_kernel(x_ref, yf_ref, gy_ref, mod_ref, wglu_ref, bglu_ref, gf_ref, gss_ref,
                 wout_ref, gmlp_ref, w1_ref, b1_ref, w2_ref, b2_ref, gfin_ref, o_ref, gs_scr):
    sub = pl.program_id(1) % TAIL_GROUP
    nck = gy_ref.shape[1]

    @pl.when(sub == 0)
    def _():
        def per_group(g, carry):
            for hh in range(SSM_GROUP):
                gs_scr[pl.ds(g * SSM_GROUP + hh, nck, stride=ZS_PITCH), :] = \
                    gy_ref[g, :, hh * SSM_CHUNK:(hh + 1) * SSM_CHUNK]
            return carry
        lax.fori_loop(0, N_GROUPS, per_group, 0)

    per_tile = nck // TAIL_GROUP
    slabs = []
    for c in range(per_tile):
        r0 = pl.multiple_of((sub * per_tile + c) * ZS_PITCH, SUBLANES)
        slabs.append(gs_scr[pl.ds(r0, D_SSM), :].T)
    gs = jnp.concatenate(slabs, axis=0).astype(_BF16)

    gate1 = mod_ref[2:3, :]
    shift2 = mod_ref[3:4, :]
    scale2 = mod_ref[4:5, :]
    gate2 = mod_ref[5:6, :]

    ab = jnp.dot(gs, wglu_ref[...], preferred_element_type=_F32) + bglu_ref[...]
    ys = ab[:, :D_SSM] * jax.nn.sigmoid(ab[:, D_SSM:])
    ysn = _rms(ys) * gss_ref[...]
    yfn = _rms(yf_ref[...].astype(_F32)) * gf_ref[...]
    merged = jnp.concatenate([yfn, ysn], axis=-1).astype(_BF16)
    x1 = x_ref[...] + gate1 * jnp.dot(merged, wout_ref[...], preferred_element_type=_F32)

    h2 = (_rms(x1) * gmlp_ref[...] * (1.0 + scale2) + shift2).astype(_BF16)
    ff = jnp.zeros(x1.shape, _F32)
    for c0 in range(0, D_FF, FF_CHUNK):
        a = jnp.dot(h2, w1_ref[:, c0:c0 + FF_CHUNK], preferred_element_type=_F32)
        a = jnp.maximum(a + b1_ref[:, c0:c0 + FF_CHUNK], 0.0)
        a = (a * a).astype(_BF16)
        ff = ff + jnp.dot(a, w2_ref[c0:c0 + FF_CHUNK, :], preferred_element_type=_F32)
    x2 = x1 + gate2 * (ff + b2_ref[...])
    o_ref[...] = _rms(x2) * gfin_ref[...]


def _tail(x, yf, gy, mod, wglu, b_glu, g_f, g_s, wout, g_mlp, w1, b1, w2, b2, g_final):
    b, l, _ = x.shape
    tm = TOKEN_TILE
    nck = TAIL_GROUP * tm // SSM_CHUNK
    blocks = l // (TAIL_GROUP * tm)
    tok = lambda width: pl.BlockSpec((None, tm, width), lambda i, j: (i, j, 0))
    row = lambda a: a.reshape(1, -1)
    single = lambda shape: pl.BlockSpec(shape, lambda *_: (0,) * len(shape),
                                        pipeline_mode=pl.Buffered(1))
    return pl.pallas_call(
        _tail_kernel,
        grid=(b, l // tm),
        in_specs=[
            tok(D_MODEL), tok(D_FOURIER),
            pl.BlockSpec((N_GROUPS, nck, CHUNK_LANES),
                         lambda i, j: (0, i * blocks + j // TAIL_GROUP, 0)),
            pl.BlockSpec((None, N_MOD, D_MODEL), lambda i, j: (i, 0, 0)),
            single((D_SSM, 2 * D_SSM)), single((1, 2 * D_SSM)),
            single((1, D_FOURIER)), single((1, D_SSM)),
            single((D_MODEL, D_MODEL)), single((1, D_MODEL)),
            single((D_MODEL, D_FF)), single((1, D_FF)),
            single((D_FF, D_MODEL)), single((1, D_MODEL)), single((1, D_MODEL)),
        ],
        out_specs=tok(D_MODEL),
        out_shape=jax.ShapeDtypeStruct((b, l, D_MODEL), _F32),
        scratch_shapes=[pltpu.VMEM((nck * ZS_PITCH, LANES), _F32)],
        compiler_params=pltpu.CompilerParams(
            dimension_semantics=("arbitrary", "arbitrary"), vmem_limit_bytes=VMEM_LIMIT),
        name="tail",
    )(x, yf, gy, mod, wglu, row(b_glu), row(g_f), row(g_s), wout, row(g_mlp),
      w1, row(b1), w2, row(b2), row(g_final))


def kernel(x_prompt, x_sample, c_prompt, c_sample, w_ada, b_ada, g_mix_norm, w_in, w_fourier, b_fourier, ssm_a_re, ssm_a_im, ssm_log_dt, ssm_b_re, ssm_b_im, ssm_c_re, ssm_c_im, ssm_d, w_glu, b_glu, g_fourier_out, g_ssm_out, w_out, g_mlp_norm, w_mlp_in, b_mlp_in, w_mlp_out, b_mlp_out, g_final):
    assert w_ada.shape[0] == 1, "single-layer block"
    n_p = c_prompt.shape[0]
    n_s = c_sample.shape[0]
    pad = (-(n_p + n_s)) % 8
    c_all = jnp.concatenate(
        [c_prompt, c_sample, jnp.zeros((pad, D_MODEL), c_prompt.dtype)], axis=0)
    mod = _adaln(c_all, w_ada[0], b_ada[0]).reshape(-1, N_MOD, D_MODEL)

    wf_b = w_in[0][:, :D_FOURIER].astype(_BF16)
    wst_b = w_in[0][:, D_FOURIER:].T.astype(_BF16)
    wcomb = _fold_fourier_weight(w_fourier[0])
    wglu_b = w_glu[0].astype(_BF16)
    wout_b = w_out[0].astype(_BF16)
    w1_b = w_mlp_in[0].astype(_BF16)
    w2_b = w_mlp_out[0].astype(_BF16)
    mod_p = mod[:n_p]
    mod_s = mod[n_p:n_p + n_s]

    x2d_p, u_p = _inproj(x_prompt, mod_p, g_mix_norm[0], wf_b, wst_b)
    x2d_s, u_s = _inproj(x_sample, mod_s, g_mix_norm[0], wf_b, wst_b)
    gy_p, gy_s = _ssm_mixer(u_p, u_s, x_prompt.shape[1] // SSM_CHUNK,
                            x_sample.shape[1] // SSM_CHUNK, ssm_a_re[0], ssm_a_im[0],
                            ssm_log_dt[0], ssm_b_re[0], ssm_b_im[0], ssm_c_re[0],
                            ssm_c_im[0], ssm_d[0])

    def finish(x, m, x2d, gy):
        yf = _fourier_mixer(x2d, wcomb, b_fourier[0])
        return _tail(x, yf, gy, m, wglu_b, b_glu[0], g_fourier_out[0], g_ssm_out[0],
                     wout_b, g_mlp_norm[0], w1_b, b_mlp_in[0], w2_b, b_mlp_out[0], g_final)

    return (finish(x_prompt, mod_p, x2d_p, gy_p), finish(x_sample, mod_s, x2d_s, gy_s))
```

```python
import functools
import math

import numpy as np
import jax
import jax.numpy as jnp
from jax import lax
from jax.experimental import pallas as pl
from jax.experimental.pallas import tpu as pltpu

D_MODEL = 1024
D_FOURIER = 512
D_SSM = 512
N_HEADS = 4
HEAD_DIM = 128
SSM_GROUP = 16
N_GROUPS = 32
SSM_STATE = 64
D_FF = 4096
N_MOD = 6
EPS = 1e-6
A_RE_MAX = -1e-4

SSM_CHUNK = 128
CHUNK_LANES = SSM_CHUNK * SSM_GROUP
STATE_LANES = 2 * SSM_STATE
W_COL_BLOCKS = 4
DFT_N1 = 128
DFT_GROUP_ROWS = 128
LANES = 128
SUBLANES = 8
ROW_PAD = 8
ZS_PITCH = D_SSM + ROW_PAD
INPROJ_TILE = 1024
TOKEN_TILE = 512
TAIL_GROUP = INPROJ_TILE // TOKEN_TILE
FF_CHUNK = 1024
VMEM_LIMIT = 56 * 1024 * 1024

_BF16 = jnp.bfloat16
_F32 = jnp.float32


def _rms(v):
    return v * lax.rsqrt(jnp.mean(v * v, axis=-1, keepdims=True) + EPS)


def _const_spec(shape):
    nd = len(shape)
    return pl.BlockSpec(shape, lambda *_: (0,) * nd)


def _adaln_kernel(c_ref, w_ref, b_ref, o_ref):
    c = c_ref[...]
    s = c * jax.nn.sigmoid(c)
    o_ref[...] = jnp.dot(s, w_ref[...], preferred_element_type=_F32,
                         precision=lax.Precision.HIGHEST) + b_ref[...]


def _adaln(c_all, w_ada, b_ada):
    rows = c_all.shape[0]
    return pl.pallas_call(
        _adaln_kernel,
        grid=(N_MOD,),
        in_specs=[
            _const_spec((rows, D_MODEL)),
            pl.BlockSpec((D_MODEL, D_MODEL), lambda j: (0, j)),
            pl.BlockSpec((1, D_MODEL), lambda j: (0, j)),
        ],
        out_specs=pl.BlockSpec((rows, D_MODEL), lambda j: (0, j)),
        out_shape=jax.ShapeDtypeStruct((rows, N_MOD * D_MODEL), _F32),
        name="adaln",
    )(c_all, w_ada, b_ada.reshape(1, -1))


def _fold_kernel(cs_ref, wf_ref, o_ref):
    o_ref[...] = jnp.dot(cs_ref[...], wf_ref[...], preferred_element_type=_F32,
                         precision=lax.Precision.HIGHEST).astype(o_ref.dtype)


@functools.lru_cache(maxsize=None)
def _channel_dft_np():
    k = np.arange(D_FOURIER)
    ang = 2.0 * np.pi * ((k[:, None] * k[None, :]) % D_FOURIER) / D_FOURIER
    scale = 1.0 / math.sqrt(D_FOURIER)
    return np.concatenate([np.cos(ang), np.sin(ang)], axis=0).astype(np.float32) * scale


def _fold_fourier_weight(w_fourier):
    wblk = jnp.zeros((D_FOURIER, D_FOURIER), _F32)
    for h in range(N_HEADS):
        sl = slice(h * HEAD_DIM, (h + 1) * HEAD_DIM)
        wblk = wblk.at[sl, sl].set(w_fourier[h])
    cs = jnp.asarray(_channel_dft_np())
    return pl.pallas_call(
        _fold_kernel,
        out_shape=jax.ShapeDtypeStruct((2 * D_FOURIER, D_FOURIER), _BF16),
        name="fold_fourier_weight",
    )(cs, wblk)


def _inproj_kernel(x_ref, mod_ref, g_ref, wf_ref, wst_ref, x2d_ref, u_ref, zf_scr, zs_scr,
                   *, n2):
    x = x_ref[...]
    tm = x.shape[0]
    h = (_rms(x) * g_ref[...] * (1.0 + mod_ref[1:2, :]) + mod_ref[0:1, :]).astype(_BF16)
    zf = jnp.dot(h, wf_ref[...], preferred_element_type=_F32)
    zst = lax.dot_general(wst_ref[...], h, (((1,), (1,)), ((), ())),
                          preferred_element_type=_F32)

    r1 = tm // n2
    pitch = n2 + ROW_PAD
    for q in range(D_FOURIER // LANES):
        for blk in range(r1):
            zf_scr[q, blk * pitch:blk * pitch + n2, :] = \
                zf[blk * n2:(blk + 1) * n2, q * LANES:(q + 1) * LANES]
    for j2 in range(n2):
        for a in range(r1 // SUBLANES):
            for q in range(D_FOURIER // LANES):
                c0 = j2 * D_FOURIER + q * LANES
                x2d_ref[SUBLANES * a:SUBLANES * (a + 1), c0:c0 + LANES] = \
                    zf_scr[q, pl.ds(j2 + SUBLANES * a * pitch, SUBLANES, stride=pitch), :]

    nck = tm // SSM_CHUNK
    for c in range(nck):
        zs_scr[c * ZS_PITCH:c * ZS_PITCH + D_SSM, :] = zst[:, c * SSM_CHUNK:(c + 1) * SSM_CHUNK]

    for g in range(N_GROUPS):
        for hh in range(SSM_GROUP):
            u_ref[g, :, hh * SSM_CHUNK:(hh + 1) * SSM_CHUNK] = \
                zs_scr[pl.ds(g * SSM_GROUP + hh, nck, stride=ZS_PITCH), :]


def _inproj(x, mod, g_mix, wf_bf16, wst_bf16):
    b, l, _ = x.shape
    tm = INPROJ_TILE
    n2 = l // DFT_N1
    r1 = tm // n2
    nck = tm // SSM_CHUNK
    steps = l // tm
    return pl.pallas_call(
        functools.partial(_inproj_kernel, n2=n2),
        grid=(b, steps),
        in_specs=[
            pl.BlockSpec((None, tm, D_MODEL), lambda i, j: (i, j, 0)),
            pl.BlockSpec((None, N_MOD, D_MODEL), lambda i, j: (i, 0, 0)),
            _const_spec((1, D_MODEL)),
            _const_spec((D_MODEL, D_FOURIER)),
            _const_spec((D_SSM, D_MODEL)),
        ],
        out_specs=[
            pl.BlockSpec((None, r1, n2 * D_FOURIER), lambda i, j: (i, j, 0)),
            pl.BlockSpec((N_GROUPS, nck, CHUNK_LANES), lambda i, j: (0, i * steps + j, 0)),
        ],
        out_shape=[
            jax.ShapeDtypeStruct((b, DFT_N1, n2 * D_FOURIER), _F32),
            jax.ShapeDtypeStruct((N_GROUPS, b * l // SSM_CHUNK, CHUNK_LANES), _F32),
        ],
        scratch_shapes=[
            pltpu.VMEM((D_FOURIER // LANES, r1 * (n2 + ROW_PAD), LANES), _F32),
            pltpu.VMEM((nck * ZS_PITCH, LANES), _F32),
        ],
        compiler_params=pltpu.CompilerParams(vmem_limit_bytes=VMEM_LIMIT),
        name="inproj",
    )(x, mod, g_mix.reshape(1, -1), wf_bf16, wst_bf16)


@functools.lru_cache(maxsize=None)
def _dft_tables_np(n):
    n1 = DFT_N1
    n2 = n // n1
    i1 = np.arange(n1)
    ang1 = 2.0 * np.pi * ((i1[:, None] * i1[None, :]) % n1) / n1
    f1 = np.concatenate([np.cos(ang1), -np.sin(ang1)], axis=0).astype(np.float32)
    i2 = np.arange(n2)
    m = (i2[None, None, :] * (i1[:, None, None] + n1 * i2[None, :, None])) % n
    ang = 2.0 * np.pi * m / n
    gr = np.cos(ang) / math.sqrt(n)
    gi = -np.sin(ang) / math.sqrt(n)
    gsz = max(1, DFT_GROUP_ROWS // n2)
    eye = np.eye(gsz)
    bd = lambda m: np.einsum("qjkn,jl->qjkln", m.reshape(n1 // gsz, gsz, n2, n2),
                             eye).reshape(n1 // gsz, gsz * n2, gsz * n2)
    gr, gi = bd(gr), bd(gi)
    g = np.concatenate([np.concatenate([gr, -gi], axis=2),
                        np.concatenate([gi, gr], axis=2)], axis=1).astype(np.float32)
    return f1, g


def _dft1_kernel(f_ref, x_ref, a_ref):
    a_ref[...] = jnp.dot(f_ref[...], x_ref[...].astype(_BF16),
                         preferred_element_type=_F32).astype(_BF16)


def _dft_stage1(x2d, f1):
    b, n1, cols = x2d.shape
    cw = 4096
    return pl.pallas_call(
        _dft1_kernel,
        grid=(b, cols // cw),
        in_specs=[
            _const_spec((2 * n1, n1)),
            pl.BlockSpec((None, n1, cw), lambda i, j: (i, 0, j)),
        ],
        out_specs=pl.BlockSpec((None, 2 * n1, cw), lambda i, j: (i, 0, j)),
        out_shape=jax.ShapeDtypeStruct((b, 2 * n1, cols), _BF16),
        name="dft_stage1",
    )(f1, x2d)


def _dft2_kernel(a_ref, g_ref, wc_ref, bf_ref, o_ref, *, kb, n2):
    gsz = kb // g_ref.shape[0]
    rows = gsz * n2
    xr, xi = [], []
    for q in range(kb // gsz):
        ks = slice(q * gsz, (q + 1) * gsz)
        aa = jnp.concatenate([a_ref[0, ks].reshape(rows, D_FOURIER),
                              a_ref[1, ks].reshape(rows, D_FOURIER)], axis=0)
        x = jnp.dot(g_ref[q], aa, preferred_element_type=_F32)
        xr.append(x[:rows].astype(_BF16))
        xi.append(x[rows:].astype(_BF16))
    xr = jnp.concatenate(xr, axis=0)
    xi = jnp.concatenate(xi, axis=0)
    y = jnp.dot(xr, wc_ref[:D_FOURIER, :], preferred_element_type=_F32)
    y = y + jnp.dot(xi, wc_ref[D_FOURIER:, :], preferred_element_type=_F32)
    y = y + bf_ref[...]
    o_ref[...] = y.reshape(kb, n2, D_FOURIER).astype(o_ref.dtype)


def _dft_stage2(a, g, wcomb, b_fourier):
    b = a.shape[0]
    n1 = DFT_N1
    n2 = a.shape[2] // D_FOURIER
    gsz = n1 // g.shape[0]
    kb = max(8, 1024 // n2)
    a5 = a.reshape(b, 2, n1, n2, D_FOURIER)
    return pl.pallas_call(
        functools.partial(_dft2_kernel, kb=kb, n2=n2),
        grid=(b, n1 // kb),
        in_specs=[
            pl.BlockSpec((None, 2, kb, n2, D_FOURIER), lambda i, j: (i, 0, j, 0, 0)),
            pl.BlockSpec((kb // gsz, 2 * gsz * n2, 2 * gsz * n2), lambda i, j: (j, 0, 0)),
            _const_spec((2 * D_FOURIER, D_FOURIER)),
            _const_spec((1, D_FOURIER)),
        ],
        out_specs=pl.BlockSpec((None, kb, n2, D_FOURIER), lambda i, j: (i, j, 0, 0)),
        out_shape=jax.ShapeDtypeStruct((b, n1, n2, D_FOURIER), _BF16),
        name="dft_stage2",
    )(a5, g, wcomb, b_fourier.reshape(1, -1))


def _fourier_mixer(x2d, wcomb, b_fourier):
    b = x2d.shape[0]
    l = x2d.shape[1] * x2d.shape[2] // D_FOURIER
    f1_np, g_np = _dft_tables_np(l)
    f1 = jnp.asarray(f1_np).astype(_BF16)
    g = jnp.asarray(g_np).astype(_BF16)
    a = _dft_stage1(x2d, f1)
    y = _dft_stage2(a, g, wcomb, b_fourier)
    return jnp.swapaxes(y, 1, 2).reshape(b, l, D_FOURIER)


def _cpow(k, zr, zi):
    mag = jnp.exp(k * zr)
    return mag * jnp.cos(k * zi), mag * jnp.sin(k * zi)


_GELU_C1 = -2.0 * math.sqrt(2.0 / math.pi) * math.log2(math.e)
_GELU_C2 = 0.044715 * _GELU_C1


def _gelu_tanh(x):
    return x / (1.0 + jnp.exp2(x * (_GELU_C1 + _GELU_C2 * (x * x))))


def _dot3(a, b):
    a_hi = a.astype(_BF16)
    b_hi = b.astype(_BF16)
    a_lo = (a - a_hi.astype(_F32)).astype(_BF16)
    b_lo = (b - b_hi.astype(_F32)).astype(_BF16)
    dot = functools.partial(jnp.dot, preferred_element_type=_F32)
    return dot(a_hi, b_hi) + (dot(a_hi, b_lo) + dot(a_lo, b_hi))


def _discretise(a_re, a_im, log_dt):
    lam_re = jnp.minimum(a_re, A_RE_MAX)
    dt = jnp.exp(log_dt)
    return lam_re, a_im, lam_re * dt, a_im * dt


def _ssm_kernel(arow_ref, acol_ref, bt_ref, ct_ref, d_ref, up_ref, us_ref,
                op_ref, os_ref, w_scr, bm_scr, e_scr, ub_scr, hc_scr, *, chunks_p, chunks_s):
    t = SSM_CHUNK
    p = SSM_STATE
    lane_k = lax.broadcasted_iota(jnp.int32, (p, t), 1).astype(_F32)
    sub_k = lax.broadcasted_iota(jnp.int32, (t, p), 0).astype(_F32)
    lane0 = lax.broadcasted_iota(jnp.int32, (p, t), 1) == 0

    zrow, bb = [], []
    for d in range(2):
        lam_re, lam_im, zr, zi = _discretise(arow_ref[d, 0:1, :], arow_ref[d, 1:2, :],
                                             arow_ref[d, 2:3, :])
        th = jnp.tanh(0.5 * zr)
        em1 = 2.0 * th / (1.0 - th)
        e_re = em1 * jnp.cos(zi) - 2.0 * jnp.square(jnp.sin(0.5 * zi))
        e_im = (em1 + 1.0) * jnp.sin(zi)
        den = lam_re * lam_re + lam_im * lam_im
        q_re = (e_re * lam_re + e_im * lam_im) / den
        q_im = (e_im * lam_re - e_re * lam_im) / den
        b_re = bt_ref[d, 0]
        b_im = bt_ref[d, 1]
        bb.append((q_re * b_re - q_im * b_im, q_re * b_im + q_im * b_re))
        zrow.append((zr, zi))

    def c_times(d, pw_re, pw_im):
        re, im = [], []
        for h in range(SSM_GROUP):
            c_re = jnp.broadcast_to(ct_ref[d, 0][:, h:h + 1], (p, t))
            c_im = jnp.broadcast_to(ct_ref[d, 1][:, h:h + 1], (p, t))
            re.append(c_re * pw_re - c_im * pw_im)
            im.append(c_re * pw_im + c_im * pw_re)
        return jnp.concatenate(re, axis=1), jnp.concatenate(im, axis=1)

    _, _, zr0, zi0 = _discretise(acol_ref[0, 0], acol_ref[0, 1], acol_ref[0, 2])
    _, _, zr1, zi1 = _discretise(acol_ref[1, 0], acol_ref[1, 1], acol_ref[1, 2])
    caf_re, caf_im = c_times(0, *_cpow(lane_k, zr0, zi0))
    cab_re, cab_im = c_times(1, *_cpow(float(t) - lane_k, zr1, zi1))
    one = jnp.where(lane0, 1.0, 0.0)
    cb0_re, cb0_im = c_times(1, one, jnp.zeros_like(one))

    e_scr[0 * p:1 * p, :] = caf_re.astype(_BF16)
    e_scr[1 * p:2 * p, :] = (-caf_im).astype(_BF16)
    e_scr[2 * p:3 * p, :] = cab_re.astype(_BF16)
    e_scr[3 * p:4 * p, :] = (-cab_im).astype(_BF16)

    (bbf_re, bbf_im), (bbb_re, bbb_im) = bb
    lhs_f = jnp.concatenate([bbf_re, -bbf_im, bbb_re, -bbb_im], axis=1)
    rhs_f = jnp.concatenate([caf_re, caf_im, cb0_re, cb0_im], axis=0)
    kf = _dot3(lhs_f, rhs_f)
    lhs_b = jnp.concatenate([bbb_re, -bbb_im], axis=1)
    rhs_b = jnp.concatenate([cab_re, cab_im], axis=0)
    kb = _dot3(lhs_b, rhs_b)

    pf_re, pf_im = _cpow(float(t - 1) - sub_k, *zrow[0])
    pb_re, pb_im = _cpow(sub_k, *zrow[1])
    for hp in range(SSM_GROUP):
        f_re, f_im = bbf_re[hp:hp + 1, :], bbf_im[hp:hp + 1, :]
        g_re, g_im = bbb_re[hp:hp + 1, :], bbb_im[hp:hp + 1, :]
        blk = jnp.concatenate([pf_re * f_re - pf_im * f_im, pf_re * f_im + pf_im * f_re,
                               pb_re * g_re - pb_im * g_im, pb_re * g_im + pb_im * g_re],
                              axis=1)
        bm_scr[hp * t:(hp + 1) * t, :] = blk.astype(_BF16)

    sl = STATE_LANES
    rows_p = up_ref.shape[0]
    ub_scr[:rows_p, :] = up_ref[...].astype(_BF16)
    ub_scr[rows_p:, :] = us_ref[...].astype(_BF16)
    s_all = jnp.dot(ub_scr[...], bm_scr[...], preferred_element_type=_F32)

    def cmul(x, ar, ai):
        a1 = jnp.concatenate([ar, ar], axis=1)
        a2 = jnp.concatenate([-ai, ai], axis=1)
        return x * a1 + pltpu.roll(x, SSM_STATE, 1) * a2

    n_levels = max(chunks_p, chunks_s).bit_length() - 1
    chunk_pow = []
    for d in range(2):
        ar, ai = _cpow(float(t), *zrow[d])
        levels = [(ar, ai)]
        for _ in range(n_levels - 1):
            ar, ai = ar * ar - ai * ai, 2.0 * ar * ai
            levels.append((ar, ai))
        chunk_pow.append(levels)

    def carry_scan(s, n_chunks, r0):
        rows = s.shape[0]
        pos = lax.broadcasted_iota(jnp.int32, (rows, sl), 0) & (n_chunks - 1)

        def shift_rows(x, n, down):
            if down:
                return pltpu.roll(x, n, 0) * (pos >= n).astype(_F32)
            return pltpu.roll(x, rows - n, 0) * (pos < n_chunks - n).astype(_F32)

        xf = s[:, :sl]
        xb = s[:, sl:]
        dist, level = 1, 0
        while dist < n_chunks:
            xf = xf + cmul(shift_rows(xf, dist, True), *chunk_pow[0][level])
            xb = xb + cmul(shift_rows(xb, dist, False), *chunk_pow[1][level])
            dist *= 2
            level += 1
        hf = cmul(shift_rows(xf, 1, True), *_cpow(1.0, *zrow[0]))
        hc_scr[r0:r0 + rows, :sl] = hf.astype(_BF16)
        hc_scr[r0:r0 + rows, sl:] = shift_rows(xb, 1, False).astype(_BF16)

    carry_scan(s_all[:rows_p], chunks_p, 0)
    carry_scan(s_all[rows_p:], chunks_s, rows_p)

    hb = SSM_GROUP // W_COL_BLOCKS
    for nb in range(W_COL_BLOCKS):
        slot = nb % 2
        for hp in range(SSM_GROUP):
            for hl in range(hb):
                h = nb * hb + hl
                lag = jnp.concatenate([kf[hp:hp + 1, h * t:(h + 1) * t],
                                       kb[hp:hp + 1, h * t:(h + 1) * t]], axis=1)
                skew = pltpu.roll(jnp.broadcast_to(lag, (t, 2 * t)), 0, 1,
                                  stride=1, stride_axis=0)
                w_scr[slot, hp * t:(hp + 1) * t, hl * t:(hl + 1) * t] = \
                    skew[:, :t].astype(_BF16)
        c0, c1 = nb * hb * t, (nb + 1) * hb * t
        y = jnp.dot(ub_scr[...], w_scr[slot], preferred_element_type=_F32)
        y = y + jnp.dot(hc_scr[...], e_scr[:, c0:c1], preferred_element_type=_F32)
        yp = y[:rows_p] + d_ref[:, c0:c1] * up_ref[:, c0:c1]
        ys = y[rows_p:] + d_ref[:, c0:c1] * us_ref[:, c0:c1]
        op_ref[:, c0:c1] = _gelu_tanh(yp)
        os_ref[:, c0:c1] = _gelu_tanh(ys)


def _ssm_mixer(up, us, chunks_p, chunks_s, a_re, a_im, log_dt, b_re, b_im, c_re, c_im, d_skip):
    g, p, h, t = N_GROUPS, SSM_STATE, SSM_GROUP, SSM_CHUNK
    ldt = jnp.broadcast_to(log_dt[:, :, None], (2, g, p))
    arow = jnp.stack([a_re, a_im, ldt] + [jnp.zeros_like(a_re)] * 5, axis=2)
    arow = jnp.transpose(arow, (1, 0, 2, 3)).astype(_F32)
    acol = jnp.stack([a_re, a_im, ldt], axis=2)
    acol = jnp.transpose(acol, (1, 0, 2, 3)).astype(_F32)
    acol = jnp.broadcast_to(acol[..., None], (g, 2, 3, p, t))
    bt = jnp.transpose(jnp.stack([b_re, b_im], axis=2), (1, 0, 2, 4, 3)).astype(_F32)
    ct = jnp.transpose(jnp.stack([c_re, c_im], axis=2), (1, 0, 2, 4, 3)).astype(_F32)
    dfl = jnp.repeat(d_skip.astype(_F32).reshape(g, 1, h), t, axis=2)

    rows_p, rows_s = up.shape[1], us.shape[1]
    grp = lambda *tail: pl.BlockSpec((None,) + tail, lambda i: (i,) + (0,) * len(tail))
    return pl.pallas_call(
        functools.partial(_ssm_kernel, chunks_p=chunks_p, chunks_s=chunks_s),
        grid=(g,),
        in_specs=[grp(2, 8, p), grp(2, 3, p, t), grp(2, 2, h, p), grp(2, 2, p, h),
                  grp(1, CHUNK_LANES), grp(rows_p, CHUNK_LANES), grp(rows_s, CHUNK_LANES)],
        out_specs=[grp(rows_p, CHUNK_LANES), grp(rows_s, CHUNK_LANES)],
        out_shape=[jax.ShapeDtypeStruct((g, rows_p, CHUNK_LANES), _F32),
                   jax.ShapeDtypeStruct((g, rows_s, CHUNK_LANES), _F32)],
        scratch_shapes=[
            pltpu.VMEM((2, CHUNK_LANES, CHUNK_LANES // W_COL_BLOCKS), _BF16),
            pltpu.VMEM((CHUNK_LANES, 2 * STATE_LANES), _BF16),
            pltpu.VMEM((2 * STATE_LANES, CHUNK_LANES), _BF16),
            pltpu.VMEM((rows_p + rows_s, CHUNK_LANES), _BF16),
            pltpu.VMEM((rows_p + rows_s, 2 * STATE_LANES), _BF16),
        ],
        compiler_params=pltpu.CompilerParams(vmem_limit_bytes=VMEM_LIMIT),
        name="ssm_chunked",
    )(arow, acol, bt, ct, dfl, up, us)


def _tail_kernel(x_ref, yf_ref, gy_ref, mod_ref, wglu_ref, bglu_ref, gf_ref, gss_ref,
                 wout_ref, gmlp_ref, w1_ref, b1_ref, w2_ref, b2_ref, gfin_ref, o_ref, gs_scr):
    sub = pl.program_id(1) % TAIL_GROUP
    nck = gy_ref.shape[1]

    @pl.when(sub == 0)
    def _():
        def per_group(g, carry):
            for hh in range(SSM_GROUP):
                gs_scr[pl.ds(g * SSM_GROUP + hh, nck, stride=ZS_PITCH), :] = \
                    gy_ref[g, :, hh * SSM_CHUNK:(hh + 1) * SSM_CHUNK]
            return carry
        lax.fori_loop(0, N_GROUPS, per_group, 0)

    per_tile = nck // TAIL_GROUP
    slabs = []
    for c in range(per_tile):
        r0 = pl.multiple_of((sub * per_tile + c) * ZS_PITCH, SUBLANES)
        slabs.append(gs_scr[pl.ds(r0, D_SSM), :].T)
    gs = jnp.concatenate(slabs, axis=0).astype(_BF16)

    gate1 = mod_ref[2:3, :]
    shift2 = mod_ref[3:4, :]
    scale2 = mod_ref[4:5, :]
    gate2 = mod_ref[5:6, :]

    ab = jnp.dot(gs, wglu_ref[...], preferred_element_type=_F32) + bglu_ref[...]
    ys = ab[:, :D_SSM] * jax.nn.sigmoid(ab[:, D_SSM:])
    ysn = _rms(ys) * gss_ref[...]
    yfn = _rms(yf_ref[...].astype(_F32)) * gf_ref[...]
    merged = jnp.concatenate([yfn, ysn], axis=-1).astype(_BF16)
    x1 = x_ref[...] + gate1 * jnp.dot(merged, wout_ref[...], preferred_element_type=_F32)

    h2 = (_rms(x1) * gmlp_ref[...] * (1.0 + scale2) + shift2).astype(_BF16)
    ff = jnp.zeros(x1.shape, _F32)
    for c0 in range(0, D_FF, FF_CHUNK):
        a = jnp.dot(h2, w1_ref[:, c0:c0 + FF_CHUNK], preferred_element_type=_F32)
        a = jnp.maximum(a + b1_ref[:, c0:c0 + FF_CHUNK], 0.0)
        a = (a * a).astype(_BF16)
        ff = ff + jnp.dot(a, w2_ref[c0:c0 + FF_CHUNK, :], preferred_element_type=_F32)
    x2 = x1 + gate2 * (ff + b2_ref[...])
    o_ref[...] = _rms(x2) * gfin_ref[...]


def _tail(x, yf, gy, mod, wglu, b_glu, g_f, g_s, wout, g_mlp, w1, b1, w2, b2, g_final):
    b, l, _ = x.shape
    tm = TOKEN_TILE
    nck = TAIL_GROUP * tm // SSM_CHUNK
    blocks = l // (TAIL_GROUP * tm)
    tok = lambda width: pl.BlockSpec((None, tm, width), lambda i, j: (i, j, 0))
    row = lambda a: a.reshape(1, -1)
    single = lambda shape: pl.BlockSpec(shape, lambda *_: (0,) * len(shape),
                                        pipeline_mode=pl.Buffered(1))
    return pl.pallas_call(
        _tail_kernel,
        grid=(b, l // tm),
        in_specs=[
            tok(D_MODEL), tok(D_FOURIER),
            pl.BlockSpec((N_GROUPS, nck, CHUNK_LANES),
                         lambda i, j: (0, i * blocks + j // TAIL_GROUP, 0)),
            pl.BlockSpec((None, N_MOD, D_MODEL), lambda i, j: (i, 0, 0)),
            single((D_SSM, 2 * D_SSM)), single((1, 2 * D_SSM)),
            single((1, D_FOURIER)), single((1, D_SSM)),
            single((D_MODEL, D_MODEL)), single((1, D_MODEL)),
            single((D_MODEL, D_FF)), single((1, D_FF)),
            single((D_FF, D_MODEL)), single((1, D_MODEL)), single((1, D_MODEL)),
        ],
        out_specs=tok(D_MODEL),
        out_shape=jax.ShapeDtypeStruct((b, l, D_MODEL), _F32),
        scratch_shapes=[pltpu.VMEM((nck * ZS_PITCH, LANES), _F32)],
        compiler_params=pltpu.CompilerParams(
            dimension_semantics=("arbitrary", "arbitrary"), vmem_limit_bytes=VMEM_LIMIT),
        name="tail",
    )(x, yf, gy, mod, wglu, row(b_glu), row(g_f), row(g_s), wout, row(g_mlp),
      w1, row(b1), w2, row(b2), row(g_final))


def kernel(x_prompt, x_sample, c_prompt, c_sample, w_ada, b_ada, g_mix_norm, w_in, w_fourier, b_fourier, ssm_a_re, ssm_a_im, ssm_log_dt, ssm_b_re, ssm_b_im, ssm_c_re, ssm_c_im, ssm_d, w_glu, b_glu, g_fourier_out, g_ssm_out, w_out, g_mlp_norm, w_mlp_in, b_mlp_in, w_mlp_out, b_mlp_out, g_final):
    assert w_ada.shape[0] == 1, "single-layer block"
    n_p = c_prompt.shape[0]
    n_s = c_sample.shape[0]
    pad = (-(n_p + n_s)) % 8
    c_all = jnp.concatenate(
        [c_prompt, c_sample, jnp.zeros((pad, D_MODEL), c_prompt.dtype)], axis=0)
    mod = _adaln(c_all, w_ada[0], b_ada[0]).reshape(-1, N_MOD, D_MODEL)

    wf_b = w_in[0][:, :D_FOURIER].astype(_BF16)
    wst_b = w_in[0][:, D_FOURIER:].T.astype(_BF16)
    wcomb = _fold_fourier_weight(w_fourier[0])
    wglu_b = w_glu[0].astype(_BF16)
    wout_b = w_out[0].astype(_BF16)
    w1_b = w_mlp_in[0].astype(_BF16)
    w2_b = w_mlp_out[0].astype(_BF16)
    mod_p = mod[:n_p]
    mod_s = mod[n_p:n_p + n_s]

    x2d_p, u_p = _inproj(x_prompt, mod_p, g_mix_norm[0], wf_b, wst_b)
    x2d_s, u_s = _inproj(x_sample, mod_s, g_mix_norm[0], wf_b, wst_b)
    gy_p, gy_s = _ssm_mixer(u_p, u_s, x_prompt.shape[1] // SSM_CHUNK,
                            x_sample.shape[1] // SSM_CHUNK, ssm_a_re[0], ssm_a_im[0],
                            ssm_log_dt[0], ssm_b_re[0], ssm_b_im[0], ssm_c_re[0],
                            ssm_c_im[0], ssm_d[0])

    def finish(x, m, x2d, gy):
        yf = _fourier_mixer(x2d, wcomb, b_fourier[0])
        return _tail(x, yf, gy, m, wglu_b, b_glu[0], g_fourier_out[0], g_ssm_out[0],
                     wout_b, g_mlp_norm[0], w1_b, b_mlp_in[0], w2_b, b_mlp_out[0], g_final)

    return (finish(x_prompt, mod_p, x2d_p, gy_p), finish(x_sample, mod_s, x2d_s, gy_s))
```

```python
import functools
import math

import numpy as np
import jax
import jax.numpy as jnp
from jax import lax
from jax.experimental import pallas as pl
from jax.experimental.pallas import tpu as pltpu

D_MODEL = 1024
D_FOURIER = 512
D_SSM = 512
N_HEADS = 4
HEAD_DIM = 128
SSM_GROUP = 16
N_GROUPS = 32
SSM_STATE = 64
D_FF = 4096
N_MOD = 6
EPS = 1e-6
A_RE_MAX = -1e-4

SSM_CHUNK = 128
CHUNK_LANES = SSM_CHUNK * SSM_GROUP
STATE_LANES = 2 * SSM_STATE
W_COL_BLOCKS = 4
DFT_N1 = 128
DFT_GROUP_ROWS = 128
DFT1_COLS = 8
LANES = 128
SUBLANES = 8
ROW_PAD = 8
ZS_PITCH = D_SSM + ROW_PAD
INPROJ_TILE = 1024
INPROJ_SUB = 256
TOKEN_TILE = 512
TAIL_GROUP = INPROJ_TILE // TOKEN_TILE
FF_CHUNK = 1024
VMEM_LIMIT = 56 * 1024 * 1024

_BF16 = jnp.bfloat16
_F32 = jnp.float32


def _rms(v):
    return v * lax.rsqrt(jnp.mean(v * v, axis=-1, keepdims=True) + EPS)


def _const_spec(shape):
    nd = len(shape)
    return pl.BlockSpec(shape, lambda *_: (0,) * nd)


def _adaln_kernel(c_ref, w_ref, b_ref, o_ref):
    c = c_ref[...]
    s = c * jax.nn.sigmoid(c)
    o_ref[...] = jnp.dot(s, w_ref[...], preferred_element_type=_F32,
                         precision=lax.Precision.HIGHEST) + b_ref[...]


def _adaln(c_all, w_ada, b_ada):
    rows = c_all.shape[0]
    return pl.pallas_call(
        _adaln_kernel,
        grid=(N_MOD,),
        in_specs=[
            _const_spec((rows, D_MODEL)),
            pl.BlockSpec((D_MODEL, D_MODEL), lambda j: (0, j)),
            pl.BlockSpec((1, D_MODEL), lambda j: (0, j)),
        ],
        out_specs=pl.BlockSpec((rows, D_MODEL), lambda j: (0, j)),
        out_shape=jax.ShapeDtypeStruct((rows, N_MOD * D_MODEL), _F32),
        name="adaln",
    )(c_all, w_ada, b_ada.reshape(1, -1))


def _fold_kernel(cs_ref, wf_ref, o_ref):
    o_ref[...] = jnp.dot(cs_ref[...], wf_ref[...], preferred_element_type=_F32,
                         precision=lax.Precision.HIGHEST).astype(o_ref.dtype)


@functools.lru_cache(maxsize=None)
def _channel_dft_np():
    k = np.arange(D_FOURIER)
    ang = 2.0 * np.pi * ((k[:, None] * k[None, :]) % D_FOURIER) / D_FOURIER
    scale = 1.0 / math.sqrt(D_FOURIER)
    return np.concatenate([np.cos(ang), np.sin(ang)], axis=0).astype(np.float32) * scale


def _fold_fourier_weight(w_fourier):
    wblk = jnp.zeros((D_FOURIER, D_FOURIER), _F32)
    for h in range(N_HEADS):
        sl = slice(h * HEAD_DIM, (h + 1) * HEAD_DIM)
        wblk = wblk.at[sl, sl].set(w_fourier[h])
    cs = jnp.asarray(_channel_dft_np())
    return pl.pallas_call(
        _fold_kernel,
        out_shape=jax.ShapeDtypeStruct((2 * D_FOURIER, D_FOURIER), _BF16),
        name="fold_fourier_weight",
    )(cs, wblk)


def _inproj_kernel(x_ref, mod_ref, g_ref, wf_ref, wst_ref, zf_ref, u_ref, zs_scr):
    tm = x_ref.shape[0]
    nck = tm // SSM_CHUNK
    for s0 in range(0, tm, INPROJ_SUB):
        x = x_ref[s0:s0 + INPROJ_SUB, :]
        h = (_rms(x) * g_ref[...] * (1.0 + mod_ref[1:2, :]) + mod_ref[0:1, :]).astype(_BF16)
        zf_ref[s0:s0 + INPROJ_SUB, :] = jnp.dot(h, wf_ref[...], preferred_element_type=_F32)
        zst = lax.dot_general(wst_ref[...], h, (((1,), (1,)), ((), ())),
                              preferred_element_type=_F32)
        for c in range(s0 // SSM_CHUNK, (s0 + INPROJ_SUB) // SSM_CHUNK):
            zs_scr[c * ZS_PITCH:c * ZS_PITCH + D_SSM, :] = \
                zst[:, c * SSM_CHUNK - s0:(c + 1) * SSM_CHUNK - s0]

    for g in range(N_GROUPS):
        for hh in range(SSM_GROUP):
            u_ref[g, :, hh * SSM_CHUNK:(hh + 1) * SSM_CHUNK] = \
                zs_scr[pl.ds(g * SSM_GROUP + hh, nck, stride=ZS_PITCH), :]


def _inproj(x, mod, g_mix, wf_bf16, wst_bf16):
    b, l, _ = x.shape
    tm = INPROJ_TILE
    nck = tm // SSM_CHUNK
    steps = l // tm
    return pl.pallas_call(
        _inproj_kernel,
        grid=(b, steps),
        in_specs=[
            pl.BlockSpec((None, tm, D_MODEL), lambda i, j: (i, j, 0)),
            pl.BlockSpec((None, N_MOD, D_MODEL), lambda i, j: (i, 0, 0)),
            _const_spec((1, D_MODEL)),
            _const_spec((D_MODEL, D_FOURIER)),
            _const_spec((D_SSM, D_MODEL)),
        ],
        out_specs=[
            pl.BlockSpec((None, tm, D_FOURIER), lambda i, j: (i, j, 0)),
            pl.BlockSpec((N_GROUPS, nck, CHUNK_LANES), lambda i, j: (0, i * steps + j, 0)),
        ],
        out_shape=[
            jax.ShapeDtypeStruct((b, l, D_FOURIER), _F32),
            jax.ShapeDtypeStruct((N_GROUPS, b * l // SSM_CHUNK, CHUNK_LANES), _F32),
        ],
        scratch_shapes=[pltpu.VMEM((nck * ZS_PITCH, LANES), _F32)],
        compiler_params=pltpu.CompilerParams(vmem_limit_bytes=VMEM_LIMIT),
        name="inproj",
    )(x, mod, g_mix.reshape(1, -1), wf_bf16, wst_bf16)


@functools.lru_cache(maxsize=None)
def _dft_tables_np(n):
    n1 = DFT_N1
    n2 = n // n1
    i1 = np.arange(n1)
    ang1 = 2.0 * np.pi * ((i1[:, None] * i1[None, :]) % n1) / n1
    f1 = np.concatenate([np.cos(ang1), -np.sin(ang1)], axis=0).astype(np.float32)
    i2 = np.arange(n2)
    m = (i2[None, None, :] * (i1[:, None, None] + n1 * i2[None, :, None])) % n
    ang = 2.0 * np.pi * m / n
    gr = np.cos(ang) / math.sqrt(n)
    gi = -np.sin(ang) / math.sqrt(n)
    gsz = max(1, DFT_GROUP_ROWS // n2)
    eye = np.eye(gsz)
    bd = lambda m: np.einsum("qjkn,jl->qjkln", m.reshape(n1 // gsz, gsz, n2, n2),
                             eye).reshape(n1 // gsz, gsz * n2, gsz * n2)
    gr, gi = bd(gr), bd(gi)
    g = np.concatenate([np.concatenate([gr, -gi], axis=2),
                        np.concatenate([gi, gr], axis=2)], axis=1).astype(np.float32)
    return f1, g


def _dft1_kernel(f_ref, x_ref, a_ref):
    xt = jnp.swapaxes(x_ref[...], 0, 1)
    for q in range(DFT1_COLS):
        a_ref[:, q * D_FOURIER:(q + 1) * D_FOURIER] = \
            jnp.dot(f_ref[...], xt[q].astype(_BF16),
                    preferred_element_type=_F32).astype(_BF16)


def _dft_stage1(zf, f1):
    b, l, _ = zf.shape
    n1 = DFT_N1
    n2 = l // n1
    cw = DFT1_COLS * D_FOURIER
    return pl.pallas_call(
        _dft1_kernel,
        grid=(b, n2 // DFT1_COLS),
        in_specs=[
            _const_spec((2 * n1, n1)),
            pl.BlockSpec((None, n1, DFT1_COLS, D_FOURIER), lambda i, j: (i, 0, j, 0)),
        ],
        out_specs=pl.BlockSpec((None, 2 * n1, cw), lambda i, j: (i, 0, j)),
        out_shape=jax.ShapeDtypeStruct((b, 2 * n1, n2 * D_FOURIER), _BF16),
        name="dft_stage1",
    )(f1, zf.reshape(b, n1, n2, D_FOURIER))


def _dft2_kernel(a_ref, g_ref, wc_ref, bf_ref, o_ref, *, kb, n2):
    gsz = kb // g_ref.shape[0]
    rows = gsz * n2
    xr, xi = [], []
    for q in range(kb // gsz):
        ks = slice(q * gsz, (q + 1) * gsz)
        aa = jnp.concatenate([a_ref[0, ks].reshape(rows, D_FOURIER),
                              a_ref[1, ks].reshape(rows, D_FOURIER)], axis=0)
        x = jnp.dot(g_ref[q], aa, preferred_element_type=_F32)
        xr.append(x[:rows].astype(_BF16))
        xi.append(x[rows:].astype(_BF16))
    xr = jnp.concatenate(xr, axis=0)
    xi = jnp.concatenate(xi, axis=0)
    y = jnp.dot(xr, wc_ref[:D_FOURIER, :], preferred_element_type=_F32)
    y = y + jnp.dot(xi, wc_ref[D_FOURIER:, :], preferred_element_type=_F32)
    y = y + bf_ref[...]
    o_ref[...] = y.reshape(kb, n2, D_FOURIER).astype(o_ref.dtype)


def _dft_stage2(a, g, wcomb, b_fourier):
    b = a.shape[0]
    n1 = DFT_N1
    n2 = a.shape[2] // D_FOURIER
    gsz = n1 // g.shape[0]
    kb = max(8, 1024 // n2)
    a5 = a.reshape(b, 2, n1, n2, D_FOURIER)
    return pl.pallas_call(
        functools.partial(_dft2_kernel, kb=kb, n2=n2),
        grid=(b, n1 // kb),
        in_specs=[
            pl.BlockSpec((None, 2, kb, n2, D_FOURIER), lambda i, j: (i, 0, j, 0, 0)),
            pl.BlockSpec((kb // gsz, 2 * gsz * n2, 2 * gsz * n2), lambda i, j: (j, 0, 0)),
            _const_spec((2 * D_FOURIER, D_FOURIER)),
            _const_spec((1, D_FOURIER)),
        ],
        out_specs=pl.BlockSpec((None, kb, n2, D_FOURIER), lambda i, j: (i, j, 0, 0)),
        out_shape=jax.ShapeDtypeStruct((b, n1, n2, D_FOURIER), _BF16),
        name="dft_stage2",
    )(a5, g, wcomb, b_fourier.reshape(1, -1))


def _fourier_mixer(zf, wcomb, b_fourier):
    b, l, _ = zf.shape
    f1_np, g_np = _dft_tables_np(l)
    f1 = jnp.asarray(f1_np).astype(_BF16)
    g = jnp.asarray(g_np).astype(_BF16)
    a = _dft_stage1(zf, f1)
    y = _dft_stage2(a, g, wcomb, b_fourier)
    return jnp.swapaxes(y, 1, 2).reshape(b, l, D_FOURIER)


def _cpow(k, zr, zi):
    mag = jnp.exp(k * zr)
    return mag * jnp.cos(k * zi), mag * jnp.sin(k * zi)


_GELU_C1 = -2.0 * math.sqrt(2.0 / math.pi) * math.log2(math.e)
_GELU_C2 = 0.044715 * _GELU_C1


def _gelu_tanh(x):
    return x / (1.0 + jnp.exp2(x * (_GELU_C1 + _GELU_C2 * (x * x))))


def _dot3(a, b):
    a_hi = a.astype(_BF16)
    b_hi = b.astype(_BF16)
    a_lo = (a - a_hi.astype(_F32)).astype(_BF16)
    b_lo = (b - b_hi.astype(_F32)).astype(_BF16)
    dot = functools.partial(jnp.dot, preferred_element_type=_F32)
    return dot(a_hi, b_hi) + (dot(a_hi, b_lo) + dot(a_lo, b_hi))


def _discretise(a_re, a_im, log_dt):
    lam_re = jnp.minimum(a_re, A_RE_MAX)
    dt = jnp.exp(log_dt)
    return lam_re, a_im, lam_re * dt, a_im * dt


def _ssm_kernel(arow_ref, acol_ref, bt_ref, ct_ref, d_ref, up_ref, us_ref,
                op_ref, os_ref, w_scr, bm_scr, e_scr, ub_scr, hc_scr, *, chunks_p, chunks_s):
    t = SSM_CHUNK
    p = SSM_STATE
    lane_k = lax.broadcasted_iota(jnp.int32, (p, t), 1).astype(_F32)
    sub_k = lax.broadcasted_iota(jnp.int32, (t, p), 0).astype(_F32)
    lane0 = lax.broadcasted_iota(jnp.int32, (p, t), 1) == 0

    zrow, bb = [], []
    for d in range(2):
        lam_re, lam_im, zr, zi = _discretise(arow_ref[d, 0:1, :], arow_ref[d, 1:2, :],
                                             arow_ref[d, 2:3, :])
        th = jnp.tanh(0.5 * zr)
        em1 = 2.0 * th / (1.0 - th)
        e_re = em1 * jnp.cos(zi) - 2.0 * jnp.square(jnp.sin(0.5 * zi))
        e_im = (em1 + 1.0) * jnp.sin(zi)
        den = lam_re * lam_re + lam_im * lam_im
        q_re = (e_re * lam_re + e_im * lam_im) / den
        q_im = (e_im * lam_re - e_re * lam_im) / den
        b_re = bt_ref[d, 0]
        b_im = bt_ref[d, 1]
        bb.append((q_re * b_re - q_im * b_im, q_re * b_im + q_im * b_re))
        zrow.append((zr, zi))

    def c_times(d, pw_re, pw_im):
        re, im = [], []
        for h in range(SSM_GROUP):
            c_re = jnp.broadcast_to(ct_ref[d, 0][:, h:h + 1], (p, t))
            c_im = jnp.broadcast_to(ct_ref[d, 1][:, h:h + 1], (p, t))
            re.append(c_re * pw_re - c_im * pw_im)
            im.append(c_re * pw_im + c_im * pw_re)
        return jnp.concatenate(re, axis=1), jnp.concatenate(im, axis=1)

    _, _, zr0, zi0 = _discretise(acol_ref[0, 0], acol_ref[0, 1], acol_ref[0, 2])
    _, _, zr1, zi1 = _discretise(acol_ref[1, 0], acol_ref[1, 1], acol_ref[1, 2])
    caf_re, caf_im = c_times(0, *_cpow(lane_k, zr0, zi0))
    cab_re, cab_im = c_times(1, *_cpow(float(t) - lane_k, zr1, zi1))
    one = jnp.where(lane0, 1.0, 0.0)
    cb0_re, cb0_im = c_times(1, one, jnp.zeros_like(one))

    e_scr[0 * p:1 * p, :] = caf_re.astype(_BF16)
    e_scr[1 * p:2 * p, :] = (-caf_im).astype(_BF16)
    e_scr[2 * p:3 * p, :] = cab_re.astype(_BF16)
    e_scr[3 * p:4 * p, :] = (-cab_im).astype(_BF16)

    (bbf_re, bbf_im), (bbb_re, bbb_im) = bb
    lhs_f = jnp.concatenate([bbf_re, -bbf_im, bbb_re, -bbb_im], axis=1)
    rhs_f = jnp.concatenate([caf_re, caf_im, cb0_re, cb0_im], axis=0)
    kf = _dot3(lhs_f, rhs_f)
    lhs_b = jnp.concatenate([bbb_re, -bbb_im], axis=1)
    rhs_b = jnp.concatenate([cab_re, cab_im], axis=0)
    kb = _dot3(lhs_b, rhs_b)

    pf_re, pf_im = _cpow(float(t - 1) - sub_k, *zrow[0])
    pb_re, pb_im = _cpow(sub_k, *zrow[1])
    for hp in range(SSM_GROUP):
        f_re, f_im = bbf_re[hp:hp + 1, :], bbf_im[hp:hp + 1, :]
        g_re, g_im = bbb_re[hp:hp + 1, :], bbb_im[hp:hp + 1, :]
        blk = jnp.concatenate([pf_re * f_re - pf_im * f_im, pf_re * f_im + pf_im * f_re,
                               pb_re * g_re - pb_im * g_im, pb_re * g_im + pb_im * g_re],
                              axis=1)
        bm_scr[hp * t:(hp + 1) * t, :] = blk.astype(_BF16)

    sl = STATE_LANES
    rows_p = up_ref.shape[0]
    ub_scr[:rows_p, :] = up_ref[...].astype(_BF16)
    ub_scr[rows_p:, :] = us_ref[...].astype(_BF16)
    s_all = jnp.dot(ub_scr[...], bm_scr[...], preferred_element_type=_F32)

    def cmul(x, ar, ai):
        a1 = jnp.concatenate([ar, ar], axis=1)
        a2 = jnp.concatenate([-ai, ai], axis=1)
        return x * a1 + pltpu.roll(x, SSM_STATE, 1) * a2

    n_levels = max(chunks_p, chunks_s).bit_length() - 1
    chunk_pow = []
    for d in range(2):
        ar, ai = _cpow(float(t), *zrow[d])
        levels = [(ar, ai)]
        for _ in range(n_levels - 1):
            ar, ai = ar * ar - ai * ai, 2.0 * ar * ai
            levels.append((ar, ai))
        chunk_pow.append(levels)

    def carry_scan(s, n_chunks, r0):
        rows = s.shape[0]
        pos = lax.broadcasted_iota(jnp.int32, (rows, sl), 0) & (n_chunks - 1)

        def shift_rows(x, n, down):
            if down:
                return pltpu.roll(x, n, 0) * (pos >= n).astype(_F32)
            return pltpu.roll(x, rows - n, 0) * (pos < n_chunks - n).astype(_F32)

        xf = s[:, :sl]
        xb = s[:, sl:]
        dist, level = 1, 0
        while dist < n_chunks:
            xf = xf + cmul(shift_rows(xf, dist, True), *chunk_pow[0][level])
            xb = xb + cmul(shift_rows(xb, dist, False), *chunk_pow[1][level])
            dist *= 2
            level += 1
        hf = cmul(shift_rows(xf, 1, True), *_cpow(1.0, *zrow[0]))
        hc_scr[r0:r0 + rows, :sl] = hf.astype(_BF16)
        hc_scr[r0:r0 + rows, sl:] = shift_rows(xb, 1, False).astype(_BF16)

    carry_scan(s_all[:rows_p], chunks_p, 0)
    carry_scan(s_all[rows_p:], chunks_s, rows_p)

    hb = SSM_GROUP // W_COL_BLOCKS
    for nb in range(W_COL_BLOCKS):
        slot = nb % 2
        for hp in range(SSM_GROUP):
            for hl in range(hb):
                h = nb * hb + hl
                lag = jnp.concatenate([kf[hp:hp + 1, h * t:(h + 1) * t],
                                       kb[hp:hp + 1, h * t:(h + 1) * t]], axis=1)
                skew = pltpu.roll(jnp.broadcast_to(lag, (t, 2 * t)), 0, 1,
                                  stride=1, stride_axis=0)
                w_scr[slot, hp * t:(hp + 1) * t, hl * t:(hl + 1) * t] = \
                    skew[:, :t].astype(_BF16)
        c0, c1 = nb * hb * t, (nb + 1) * hb * t
        y = jnp.dot(ub_scr[...], w_scr[slot], preferred_element_type=_F32)
        y = y + jnp.dot(hc_scr[...], e_scr[:, c0:c1], preferred_element_type=_F32)
        yp = y[:rows_p] + d_ref[:, c0:c1] * up_ref[:, c0:c1]
        ys = y[rows_p:] + d_ref[:, c0:c1] * us_ref[:, c0:c1]
        op_ref[:, c0:c1] = _gelu_tanh(yp)
        os_ref[:, c0:c1] = _gelu_tanh(ys)


def _ssm_mixer(up, us, chunks_p, chunks_s, a_re, a_im, log_dt, b_re, b_im, c_re, c_im, d_skip):
    g, p, h, t = N_GROUPS, SSM_STATE, SSM_GROUP, SSM_CHUNK
    ldt = jnp.broadcast_to(log_dt[:, :, None], (2, g, p))
    arow = jnp.stack([a_re, a_im, ldt] + [jnp.zeros_like(a_re)] * 5, axis=2)
    arow = jnp.transpose(arow, (1, 0, 2, 3)).astype(_F32)
    acol = jnp.stack([a_re, a_im, ldt], axis=2)
    acol = jnp.transpose(acol, (1, 0, 2, 3)).astype(_F32)
    acol = jnp.broadcast_to(acol[..., None], (g, 2, 3, p, t))
    bt = jnp.transpose(jnp.stack([b_re, b_im], axis=2), (1, 0, 2, 4, 3)).astype(_F32)
    ct = jnp.transpose(jnp.stack([c_re, c_im], axis=2), (1, 0, 2, 4, 3)).astype(_F32)
    dfl = jnp.repeat(d_skip.astype(_F32).reshape(g, 1, h), t, axis=2)

    rows_p, rows_s = up.shape[1], us.shape[1]
    grp = lambda *tail: pl.BlockSpec((None,) + tail, lambda i: (i,) + (0,) * len(tail))
    return pl.pallas_call(
        functools.partial(_ssm_kernel, chunks_p=chunks_p, chunks_s=chunks_s),
        grid=(g,),
        in_specs=[grp(2, 8, p), grp(2, 3, p, t), grp(2, 2, h, p), grp(2, 2, p, h),
                  grp(1, CHUNK_LANES), grp(rows_p, CHUNK_LANES), grp(rows_s, CHUNK_LANES)],
        out_specs=[grp(rows_p, CHUNK_LANES), grp(rows_s, CHUNK_LANES)],
        out_shape=[jax.ShapeDtypeStruct((g, rows_p, CHUNK_LANES), _F32),
                   jax.ShapeDtypeStruct((g, rows_s, CHUNK_LANES), _F32)],
        scratch_shapes=[
            pltpu.VMEM((2, CHUNK_LANES, CHUNK_LANES // W_COL_BLOCKS), _BF16),
            pltpu.VMEM((CHUNK_LANES, 2 * STATE_LANES), _BF16),
            pltpu.VMEM((2 * STATE_LANES, CHUNK_LANES), _BF16),
            pltpu.VMEM((rows_p + rows_s, CHUNK_LANES), _BF16),
            pltpu.VMEM((rows_p + rows_s, 2 * STATE_LANES), _BF16),
        ],
        compiler_params=pltpu.CompilerParams(vmem_limit_bytes=VMEM_LIMIT),
        name="ssm_chunked",
    )(arow, acol, bt, ct, dfl, up, us)


def _tail_kernel(x_ref, yf_ref, gy_ref, mod_ref, wglu_ref, bglu_ref, gf_ref, gss_ref,
                 wout_ref, gmlp_ref, w1_ref, b1_ref, w2_ref, b2_ref, gfin_ref, o_ref, gs_scr):
    sub = pl.program_id(1) % TAIL_GROUP
    nck = gy_ref.shape[1]

    @pl.when(sub == 0)
    def _():
        def per_group(g, carry):
            for hh in range(SSM_GROUP):
                gs_scr[pl.ds(g * SSM_GROUP + hh, nck, stride=ZS_PITCH), :] = \
                    gy_ref[g, :, hh * SSM_CHUNK:(hh + 1) * SSM_CHUNK]
            return carry
        lax.fori_loop(0, N_GROUPS, per_group, 0)

    per_tile = nck // TAIL_GROUP
    slabs = []
    for c in range(per_tile):
        r0 = pl.multiple_of((sub * per_tile + c) * ZS_PITCH, SUBLANES)
        slabs.append(gs_scr[pl.ds(r0, D_SSM), :].T)
    gs = jnp.concatenate(slabs, axis=0).astype(_BF16)

    gate1 = mod_ref[2:3, :]
    shift2 = mod_ref[3:4, :]
    scale2 = mod_ref[4:5, :]
    gate2 = mod_ref[5:6, :]

    ab = jnp.dot(gs, wglu_ref[...], preferred_element_type=_F32) + bglu_ref[...]
    ys = ab[:, :D_SSM] * jax.nn.sigmoid(ab[:, D_SSM:])
    ysn = _rms(ys) * gss_ref[...]
    yfn = _rms(yf_ref[...].astype(_F32)) * gf_ref[...]
    merged = jnp.concatenate([yfn, ysn], axis=-1).astype(_BF16)
    x1 = x_ref[...] + gate1 * jnp.dot(merged, wout_ref[...], preferred_element_type=_F32)

    h2 = (_rms(x1) * gmlp_ref[...] * (1.0 + scale2) + shift2).astype(_BF16)
    ff = jnp.zeros(x1.shape, _F32)
    for c0 in range(0, D_FF, FF_CHUNK):
        a = jnp.dot(h2, w1_ref[:, c0:c0 + FF_CHUNK], preferred_element_type=_F32)
        a = jnp.maximum(a + b1_ref[:, c0:c0 + FF_CHUNK], 0.0)
        a = (a * a).astype(_BF16)
        ff = ff + jnp.dot(a, w2_ref[c0:c0 + FF_CHUNK, :], preferred_element_type=_F32)
    x2 = x1 + gate2 * (ff + b2_ref[...])
    o_ref[...] = _rms(x2) * gfin_ref[...]


def _tail(x, yf, gy, mod, wglu, b_glu, g_f, g_s, wout, g_mlp, w1, b1, w2, b2, g_final):
    b, l, _ = x.shape
    tm = TOKEN_TILE
    nck = TAIL_GROUP * tm // SSM_CHUNK
    blocks = l // (TAIL_GROUP * tm)
    tok = lambda width: pl.BlockSpec((None, tm, width), lambda i, j: (i, j, 0))
    row = lambda a: a.reshape(1, -1)
    single = lambda shape: pl.BlockSpec(shape, lambda *_: (0,) * len(shape),
                                        pipeline_mode=pl.Buffered(1))
    return pl.pallas_call(
        _tail_kernel,
        grid=(b, l // tm),
        in_specs=[
            tok(D_MODEL), tok(D_FOURIER),
            pl.BlockSpec((N_GROUPS, nck, CHUNK_LANES),
                         lambda i, j: (0, i * blocks + j // TAIL_GROUP, 0)),
            pl.BlockSpec((None, N_MOD, D_MODEL), lambda i, j: (i, 0, 0)),
            single((D_SSM, 2 * D_SSM)), single((1, 2 * D_SSM)),
            single((1, D_FOURIER)), single((1, D_SSM)),
            single((D_MODEL, D_MODEL)), single((1, D_MODEL)),
            single((D_MODEL, D_FF)), single((1, D_FF)),
            single((D_FF, D_MODEL)), single((1, D_MODEL)), single((1, D_MODEL)),
        ],
        out_specs=tok(D_MODEL),
        out_shape=jax.ShapeDtypeStruct((b, l, D_MODEL), _F32),
        scratch_shapes=[pltpu.VMEM((nck * ZS_PITCH, LANES), _F32)],
        compiler_params=pltpu.CompilerParams(
            dimension_semantics=("arbitrary", "arbitrary"), vmem_limit_bytes=VMEM_LIMIT),
        name="tail",
    )(x, yf, gy, mod, wglu, row(b_glu), row(g_f), row(g_s), wout, row(g_mlp),
      w1, row(b1), w2, row(b2), row(g_final))


def kernel(x_prompt, x_sample, c_prompt, c_sample, w_ada, b_ada, g_mix_norm, w_in, w_fourier, b_fourier, ssm_a_re, ssm_a_im, ssm_log_dt, ssm_b_re, ssm_b_im, ssm_c_re, ssm_c_im, ssm_d, w_glu, b_glu, g_fourier_out, g_ssm_out, w_out, g_mlp_norm, w_mlp_in, b_mlp_in, w_mlp_out, b_mlp_out, g_final):
    assert w_ada.shape[0] == 1, "single-layer block"
    n_p = c_prompt.shape[0]
    n_s = c_sample.shape[0]
    pad = (-(n_p + n_s)) % 8
    c_all = jnp.concatenate(
        [c_prompt, c_sample, jnp.zeros((pad, D_MODEL), c_prompt.dtype)], axis=0)
    mod = _adaln(c_all, w_ada[0], b_ada[0]).reshape(-1, N_MOD, D_MODEL)

    wf_b = w_in[0][:, :D_FOURIER].astype(_BF16)
    wst_b = w_in[0][:, D_FOURIER:].T.astype(_BF16)
    wcomb = _fold_fourier_weight(w_fourier[0])
    wglu_b = w_glu[0].astype(_BF16)
    wout_b = w_out[0].astype(_BF16)
    w1_b = w_mlp_in[0].astype(_BF16)
    w2_b = w_mlp_out[0].astype(_BF16)
    mod_p = mod[:n_p]
    mod_s = mod[n_p:n_p + n_s]

    zf_p, u_p = _inproj(x_prompt, mod_p, g_mix_norm[0], wf_b, wst_b)
    zf_s, u_s = _inproj(x_sample, mod_s, g_mix_norm[0], wf_b, wst_b)
    gy_p, gy_s = _ssm_mixer(u_p, u_s, x_prompt.shape[1] // SSM_CHUNK,
                            x_sample.shape[1] // SSM_CHUNK, ssm_a_re[0], ssm_a_im[0],
                            ssm_log_dt[0], ssm_b_re[0], ssm_b_im[0], ssm_c_re[0],
                            ssm_c_im[0], ssm_d[0])

    def finish(x, m, zf, gy):
        yf = _fourier_mixer(zf, wcomb, b_fourier[0])
        return _tail(x, yf, gy, m, wglu_b, b_glu[0], g_fourier_out[0], g_ssm_out[0],
                     wout_b, g_mlp_norm[0], w1_b, b_mlp_in[0], w2_b, b_mlp_out[0], g_final)

    return (finish(x_prompt, mod_p, zf_p, gy_p), finish(x_sample, mod_s, zf_s, gy_s))
```

```python
import functools
import math

import numpy as np
import jax
import jax.numpy as jnp
from jax import lax
from jax.experimental import pallas as pl
from jax.experimental.pallas import tpu as pltpu

D_MODEL = 1024
D_FOURIER = 512
D_SSM = 512
N_HEADS = 4
HEAD_DIM = 128
SSM_GROUP = 16
N_GROUPS = 32
SSM_STATE = 64
D_FF = 4096
N_MOD = 6
EPS = 1e-6
A_RE_MAX = -1e-4

SSM_CHUNK = 128
CHUNK_LANES = SSM_CHUNK * SSM_GROUP
STATE_LANES = 2 * SSM_STATE
W_COL_BLOCKS = 4
DFT_N1 = 128
DFT_GROUP_ROWS = 128
DFT1_COLS = 8
LANES = 128
SUBLANES = 8
ROW_PAD = 8
ZS_PITCH = D_SSM + ROW_PAD
INPROJ_TILE = 2048
INPROJ_SUB = 256
TOKEN_TILE = 512
TAIL_GROUP = 2
FF_CHUNK = 1024
VMEM_LIMIT = 56 * 1024 * 1024

_BF16 = jnp.bfloat16
_F32 = jnp.float32


def _rms(v):
    return v * lax.rsqrt(jnp.mean(v * v, axis=-1, keepdims=True) + EPS)


def _const_spec(shape):
    nd = len(shape)
    return pl.BlockSpec(shape, lambda *_: (0,) * nd)


def _adaln_kernel(c_ref, w_ref, b_ref, o_ref):
    c = c_ref[...]
    s = c * jax.nn.sigmoid(c)
    o_ref[...] = jnp.dot(s, w_ref[...], preferred_element_type=_F32,
                         precision=lax.Precision.HIGHEST) + b_ref[...]


def _adaln(c_all, w_ada, b_ada):
    rows = c_all.shape[0]
    return pl.pallas_call(
        _adaln_kernel,
        grid=(N_MOD,),
        in_specs=[
            _const_spec((rows, D_MODEL)),
            pl.BlockSpec((D_MODEL, D_MODEL), lambda j: (0, j)),
            pl.BlockSpec((1, D_MODEL), lambda j: (0, j)),
        ],
        out_specs=pl.BlockSpec((rows, D_MODEL), lambda j: (0, j)),
        out_shape=jax.ShapeDtypeStruct((rows, N_MOD * D_MODEL), _F32),
        name="adaln",
    )(c_all, w_ada, b_ada.reshape(1, -1))


def _fold_kernel(cs_ref, wf_ref, o_ref):
    o_ref[...] = jnp.dot(cs_ref[...], wf_ref[...], preferred_element_type=_F32,
                         precision=lax.Precision.HIGHEST).astype(o_ref.dtype)


@functools.lru_cache(maxsize=None)
def _channel_dft_np():
    k = np.arange(D_FOURIER)
    ang = 2.0 * np.pi * ((k[:, None] * k[None, :]) % D_FOURIER) / D_FOURIER
    scale = 1.0 / math.sqrt(D_FOURIER)
    return np.concatenate([np.cos(ang), np.sin(ang)], axis=0).astype(np.float32) * scale


def _fold_fourier_weight(w_fourier):
    wblk = jnp.zeros((D_FOURIER, D_FOURIER), _F32)
    for h in range(N_HEADS):
        sl = slice(h * HEAD_DIM, (h + 1) * HEAD_DIM)
        wblk = wblk.at[sl, sl].set(w_fourier[h])
    cs = jnp.asarray(_channel_dft_np())
    return pl.pallas_call(
        _fold_kernel,
        out_shape=jax.ShapeDtypeStruct((2 * D_FOURIER, D_FOURIER), _BF16),
        name="fold_fourier_weight",
    )(cs, wblk)


def _inproj_kernel(x_ref, mod_ref, g_ref, wf_ref, wst_ref, zf_ref, u_ref, zs_scr):
    tm = x_ref.shape[0]
    nck = tm // SSM_CHUNK
    for s0 in range(0, tm, INPROJ_SUB):
        x = x_ref[s0:s0 + INPROJ_SUB, :]
        h = (_rms(x) * g_ref[...] * (1.0 + mod_ref[1:2, :]) + mod_ref[0:1, :]).astype(_BF16)
        zf_ref[s0:s0 + INPROJ_SUB, :] = jnp.dot(h, wf_ref[...], preferred_element_type=_F32)
        zst = lax.dot_general(wst_ref[...], h, (((1,), (1,)), ((), ())),
                              preferred_element_type=_F32)
        for c in range(s0 // SSM_CHUNK, (s0 + INPROJ_SUB) // SSM_CHUNK):
            zs_scr[c * ZS_PITCH:c * ZS_PITCH + D_SSM, :] = \
                zst[:, c * SSM_CHUNK - s0:(c + 1) * SSM_CHUNK - s0]

    for g in range(N_GROUPS):
        for hh in range(SSM_GROUP):
            u_ref[g, :, hh * SSM_CHUNK:(hh + 1) * SSM_CHUNK] = \
                zs_scr[pl.ds(g * SSM_GROUP + hh, nck, stride=ZS_PITCH), :].astype(_BF16)


def _inproj(x, mod, g_mix, wf_bf16, wst_bf16):
    b, l, _ = x.shape
    tm = INPROJ_TILE
    nck = tm // SSM_CHUNK
    steps = l // tm
    return pl.pallas_call(
        _inproj_kernel,
        grid=(b, steps),
        in_specs=[
            pl.BlockSpec((None, tm, D_MODEL), lambda i, j: (i, j, 0)),
            pl.BlockSpec((None, N_MOD, D_MODEL), lambda i, j: (i, 0, 0)),
            _const_spec((1, D_MODEL)),
            _const_spec((D_MODEL, D_FOURIER)),
            _const_spec((D_SSM, D_MODEL)),
        ],
        out_specs=[
            pl.BlockSpec((None, tm, D_FOURIER), lambda i, j: (i, j, 0)),
            pl.BlockSpec((N_GROUPS, nck, CHUNK_LANES), lambda i, j: (0, i * steps + j, 0)),
        ],
        out_shape=[
            jax.ShapeDtypeStruct((b, l, D_FOURIER), _F32),
            jax.ShapeDtypeStruct((N_GROUPS, b * l // SSM_CHUNK, CHUNK_LANES), _BF16),
        ],
        scratch_shapes=[pltpu.VMEM((nck * ZS_PITCH, LANES), _F32)],
        compiler_params=pltpu.CompilerParams(vmem_limit_bytes=VMEM_LIMIT),
        name="inproj",
    )(x, mod, g_mix.reshape(1, -1), wf_bf16, wst_bf16)


@functools.lru_cache(maxsize=None)
def _dft_tables_np(n):
    n1 = DFT_N1
    n2 = n // n1
    i1 = np.arange(n1)
    ang1 = 2.0 * np.pi * ((i1[:, None] * i1[None, :]) % n1) / n1
    f1 = np.concatenate([np.cos(ang1), -np.sin(ang1)], axis=0).astype(np.float32)
    i2 = np.arange(n2)
    m = (i2[None, None, :] * (i1[:, None, None] + n1 * i2[None, :, None])) % n
    ang = 2.0 * np.pi * m / n
    gr = np.cos(ang) / math.sqrt(n)
    gi = -np.sin(ang) / math.sqrt(n)
    gsz = max(1, DFT_GROUP_ROWS // n2)
    eye = np.eye(gsz)
    bd = lambda m: np.einsum("qjkn,jl->qjkln", m.reshape(n1 // gsz, gsz, n2, n2),
                             eye).reshape(n1 // gsz, gsz * n2, gsz * n2)
    gr, gi = bd(gr), bd(gi)
    g = np.concatenate([np.concatenate([gr, -gi], axis=2),
                        np.concatenate([gi, gr], axis=2)], axis=1).astype(np.float32)
    return f1, g


def _dft1_kernel(f_ref, x_ref, a_ref):
    xt = jnp.swapaxes(x_ref[...], 0, 1)
    for q in range(DFT1_COLS):
        a_ref[:, q * D_FOURIER:(q + 1) * D_FOURIER] = \
            jnp.dot(f_ref[...], xt[q].astype(_BF16),
                    preferred_element_type=_F32).astype(_BF16)


def _dft_stage1(zf, f1):
    b, l, _ = zf.shape
    n1 = DFT_N1
    n2 = l // n1
    cw = DFT1_COLS * D_FOURIER
    return pl.pallas_call(
        _dft1_kernel,
        grid=(b, n2 // DFT1_COLS),
        in_specs=[
            _const_spec((2 * n1, n1)),
            pl.BlockSpec((None, n1, DFT1_COLS, D_FOURIER), lambda i, j: (i, 0, j, 0)),
        ],
        out_specs=pl.BlockSpec((None, 2 * n1, cw), lambda i, j: (i, 0, j)),
        out_shape=jax.ShapeDtypeStruct((b, 2 * n1, n2 * D_FOURIER), _BF16),
        name="dft_stage1",
    )(f1, zf.reshape(b, n1, n2, D_FOURIER))


def _dft2_kernel(a_ref, g_ref, wc_ref, bf_ref, o_ref, *, kb, n2):
    gsz = kb // g_ref.shape[0]
    rows = gsz * n2
    xr, xi = [], []
    for q in range(kb // gsz):
        ks = slice(q * gsz, (q + 1) * gsz)
        aa = jnp.concatenate([a_ref[0, ks].reshape(rows, D_FOURIER),
                              a_ref[1, ks].reshape(rows, D_FOURIER)], axis=0)
        x = jnp.dot(g_ref[q], aa, preferred_element_type=_F32)
        xr.append(x[:rows].astype(_BF16))
        xi.append(x[rows:].astype(_BF16))
    xr = jnp.concatenate(xr, axis=0)
    xi = jnp.concatenate(xi, axis=0)
    y = jnp.dot(xr, wc_ref[:D_FOURIER, :], preferred_element_type=_F32)
    y = y + jnp.dot(xi, wc_ref[D_FOURIER:, :], preferred_element_type=_F32)
    y = y + bf_ref[...]
    o_ref[...] = y.reshape(kb, n2, D_FOURIER).astype(o_ref.dtype)


def _dft_stage2(a, g, wcomb, b_fourier):
    b = a.shape[0]
    n1 = DFT_N1
    n2 = a.shape[2] // D_FOURIER
    gsz = n1 // g.shape[0]
    kb = max(8, 1024 // n2)
    a5 = a.reshape(b, 2, n1, n2, D_FOURIER)
    return pl.pallas_call(
        functools.partial(_dft2_kernel, kb=kb, n2=n2),
        grid=(b, n1 // kb),
        in_specs=[
            pl.BlockSpec((None, 2, kb, n2, D_FOURIER), lambda i, j: (i, 0, j, 0, 0)),
            pl.BlockSpec((kb // gsz, 2 * gsz * n2, 2 * gsz * n2), lambda i, j: (j, 0, 0)),
            _const_spec((2 * D_FOURIER, D_FOURIER)),
            _const_spec((1, D_FOURIER)),
        ],
        out_specs=pl.BlockSpec((None, kb, n2, D_FOURIER), lambda i, j: (i, j, 0, 0)),
        out_shape=jax.ShapeDtypeStruct((b, n1, n2, D_FOURIER), _BF16),
        name="dft_stage2",
    )(a5, g, wcomb, b_fourier.reshape(1, -1))


def _fourier_mixer(zf, wcomb, b_fourier):
    b, l, _ = zf.shape
    f1_np, g_np = _dft_tables_np(l)
    f1 = jnp.asarray(f1_np).astype(_BF16)
    g = jnp.asarray(g_np).astype(_BF16)
    a = _dft_stage1(zf, f1)
    y = _dft_stage2(a, g, wcomb, b_fourier)
    return jnp.swapaxes(y, 1, 2).reshape(b, l, D_FOURIER)


def _cpow(k, zr, zi):
    mag = jnp.exp(k * zr)
    return mag * jnp.cos(k * zi), mag * jnp.sin(k * zi)


_GELU_C1 = -2.0 * math.sqrt(2.0 / math.pi) * math.log2(math.e)
_GELU_C2 = 0.044715 * _GELU_C1


def _gelu_tanh(x):
    return x / (1.0 + jnp.exp2(x * (_GELU_C1 + _GELU_C2 * (x * x))))


def _dot3(a, b):
    a_hi = a.astype(_BF16)
    b_hi = b.astype(_BF16)
    a_lo = (a - a_hi.astype(_F32)).astype(_BF16)
    b_lo = (b - b_hi.astype(_F32)).astype(_BF16)
    dot = functools.partial(jnp.dot, preferred_element_type=_F32)
    return dot(a_hi, b_hi) + (dot(a_hi, b_lo) + dot(a_lo, b_hi))


def _discretise(a_re, a_im, log_dt):
    lam_re = jnp.minimum(a_re, A_RE_MAX)
    dt = jnp.exp(log_dt)
    return lam_re, a_im, lam_re * dt, a_im * dt


def _ssm_kernel(arow_ref, acol_ref, bt_ref, ct_ref, d_ref, up_ref, us_ref,
                op_ref, os_ref, w_scr, bm_scr, e_scr, ub_scr, hc_scr, *, chunks_p, chunks_s):
    t = SSM_CHUNK
    p = SSM_STATE
    lane_k = lax.broadcasted_iota(jnp.int32, (p, t), 1).astype(_F32)
    sub_k = lax.broadcasted_iota(jnp.int32, (t, p), 0).astype(_F32)
    lane0 = lax.broadcasted_iota(jnp.int32, (p, t), 1) == 0

    zrow, bb = [], []
    for d in range(2):
        lam_re, lam_im, zr, zi = _discretise(arow_ref[d, 0:1, :], arow_ref[d, 1:2, :],
                                             arow_ref[d, 2:3, :])
        th = jnp.tanh(0.5 * zr)
        em1 = 2.0 * th / (1.0 - th)
        e_re = em1 * jnp.cos(zi) - 2.0 * jnp.square(jnp.sin(0.5 * zi))
        e_im = (em1 + 1.0) * jnp.sin(zi)
        den = lam_re * lam_re + lam_im * lam_im
        q_re = (e_re * lam_re + e_im * lam_im) / den
        q_im = (e_im * lam_re - e_re * lam_im) / den
        b_re = bt_ref[d, 0]
        b_im = bt_ref[d, 1]
        bb.append((q_re * b_re - q_im * b_im, q_re * b_im + q_im * b_re))
        zrow.append((zr, zi))

    def c_times(d, pw_re, pw_im):
        re, im = [], []
        for h in range(SSM_GROUP):
            c_re = jnp.broadcast_to(ct_ref[d, 0][:, h:h + 1], (p, t))
            c_im = jnp.broadcast_to(ct_ref[d, 1][:, h:h + 1], (p, t))
            re.append(c_re * pw_re - c_im * pw_im)
            im.append(c_re * pw_im + c_im * pw_re)
        return jnp.concatenate(re, axis=1), jnp.concatenate(im, axis=1)

    _, _, zr0, zi0 = _discretise(acol_ref[0, 0], acol_ref[0, 1], acol_ref[0, 2])
    _, _, zr1, zi1 = _discretise(acol_ref[1, 0], acol_ref[1, 1], acol_ref[1, 2])
    caf_re, caf_im = c_times(0, *_cpow(lane_k, zr0, zi0))
    cab_re, cab_im = c_times(1, *_cpow(float(t) - lane_k, zr1, zi1))
    one = jnp.where(lane0, 1.0, 0.0)
    cb0_re, cb0_im = c_times(1, one, jnp.zeros_like(one))

    e_scr[0 * p:1 * p, :] = caf_re.astype(_BF16)
    e_scr[1 * p:2 * p, :] = (-caf_im).astype(_BF16)
    e_scr[2 * p:3 * p, :] = cab_re.astype(_BF16)
    e_scr[3 * p:4 * p, :] = (-cab_im).astype(_BF16)

    (bbf_re, bbf_im), (bbb_re, bbb_im) = bb
    lhs_f = jnp.concatenate([bbf_re, -bbf_im, bbb_re, -bbb_im], axis=1)
    rhs_f = jnp.concatenate([caf_re, caf_im, cb0_re, cb0_im], axis=0)
    kf = _dot3(lhs_f, rhs_f)
    lhs_b = jnp.concatenate([bbb_re, -bbb_im], axis=1)
    rhs_b = jnp.concatenate([cab_re, cab_im], axis=0)
    kb = _dot3(lhs_b, rhs_b)

    pf_re, pf_im = _cpow(float(t - 1) - sub_k, *zrow[0])
    pb_re, pb_im = _cpow(sub_k, *zrow[1])
    for hp in range(SSM_GROUP):
        f_re, f_im = bbf_re[hp:hp + 1, :], bbf_im[hp:hp + 1, :]
        g_re, g_im = bbb_re[hp:hp + 1, :], bbb_im[hp:hp + 1, :]
        blk = jnp.concatenate([pf_re * f_re - pf_im * f_im, pf_re * f_im + pf_im * f_re,
                               pb_re * g_re - pb_im * g_im, pb_re * g_im + pb_im * g_re],
                              axis=1)
        bm_scr[hp * t:(hp + 1) * t, :] = blk.astype(_BF16)

    sl = STATE_LANES
    rows_p = up_ref.shape[0]
    ub_scr[:rows_p, :] = up_ref[...]
    ub_scr[rows_p:, :] = us_ref[...]
    s_all = jnp.dot(ub_scr[...], bm_scr[...], preferred_element_type=_F32)

    def cmul(x, ar, ai):
        a1 = jnp.concatenate([ar, ar], axis=1)
        a2 = jnp.concatenate([-ai, ai], axis=1)
        return x * a1 + pltpu.roll(x, SSM_STATE, 1) * a2

    n_levels = max(chunks_p, chunks_s).bit_length() - 1
    chunk_pow = []
    for d in range(2):
        ar, ai = _cpow(float(t), *zrow[d])
        levels = [(ar, ai)]
        for _ in range(n_levels - 1):
            ar, ai = ar * ar - ai * ai, 2.0 * ar * ai
            levels.append((ar, ai))
        chunk_pow.append(levels)

    def carry_scan(s, n_chunks, r0):
        rows = s.shape[0]
        pos = lax.broadcasted_iota(jnp.int32, (rows, sl), 0) & (n_chunks - 1)

        def shift_rows(x, n, down):
            if down:
                return pltpu.roll(x, n, 0) * (pos >= n).astype(_F32)
            return pltpu.roll(x, rows - n, 0) * (pos < n_chunks - n).astype(_F32)

        xf = s[:, :sl]
        xb = s[:, sl:]
        dist, level = 1, 0
        while dist < n_chunks:
            xf = xf + cmul(shift_rows(xf, dist, True), *chunk_pow[0][level])
            xb = xb + cmul(shift_rows(xb, dist, False), *chunk_pow[1][level])
            dist *= 2
            level += 1
        hf = cmul(shift_rows(xf, 1, True), *_cpow(1.0, *zrow[0]))
        hc_scr[r0:r0 + rows, :sl] = hf.astype(_BF16)
        hc_scr[r0:r0 + rows, sl:] = shift_rows(xb, 1, False).astype(_BF16)

    carry_scan(s_all[:rows_p], chunks_p, 0)
    carry_scan(s_all[rows_p:], chunks_s, rows_p)

    hb = SSM_GROUP // W_COL_BLOCKS
    for nb in range(W_COL_BLOCKS):
        slot = nb % 2
        for hp in range(SSM_GROUP):
            for hl in range(hb):
                h = nb * hb + hl
                lag = jnp.concatenate([kf[hp:hp + 1, h * t:(h + 1) * t],
                                       kb[hp:hp + 1, h * t:(h + 1) * t]], axis=1)
                skew = pltpu.roll(jnp.broadcast_to(lag, (t, 2 * t)), 0, 1,
                                  stride=1, stride_axis=0)
                w_scr[slot, hp * t:(hp + 1) * t, hl * t:(hl + 1) * t] = \
                    skew[:, :t].astype(_BF16)
        c0, c1 = nb * hb * t, (nb + 1) * hb * t
        y = jnp.dot(ub_scr[...], w_scr[slot], preferred_element_type=_F32)
        y = y + jnp.dot(hc_scr[...], e_scr[:, c0:c1], preferred_element_type=_F32)
        yp = y[:rows_p] + d_ref[:, c0:c1] * up_ref[:, c0:c1].astype(_F32)
        ys = y[rows_p:] + d_ref[:, c0:c1] * us_ref[:, c0:c1].astype(_F32)
        op_ref[:, c0:c1] = _gelu_tanh(yp)
        os_ref[:, c0:c1] = _gelu_tanh(ys)


def _ssm_mixer(up, us, chunks_p, chunks_s, a_re, a_im, log_dt, b_re, b_im, c_re, c_im, d_skip):
    g, p, h, t = N_GROUPS, SSM_STATE, SSM_GROUP, SSM_CHUNK
    ldt = jnp.broadcast_to(log_dt[:, :, None], (2, g, p))
    arow = jnp.stack([a_re, a_im, ldt] + [jnp.zeros_like(a_re)] * 5, axis=2)
    arow = jnp.transpose(arow, (1, 0, 2, 3)).astype(_F32)
    acol = jnp.stack([a_re, a_im, ldt], axis=2)
    acol = jnp.transpose(acol, (1, 0, 2, 3)).astype(_F32)
    acol = jnp.broadcast_to(acol[..., None], (g, 2, 3, p, t))
    bt = jnp.transpose(jnp.stack([b_re, b_im], axis=2), (1, 0, 2, 4, 3)).astype(_F32)
    ct = jnp.transpose(jnp.stack([c_re, c_im], axis=2), (1, 0, 2, 4, 3)).astype(_F32)
    dfl = jnp.repeat(d_skip.astype(_F32).reshape(g, 1, h), t, axis=2)

    rows_p, rows_s = up.shape[1], us.shape[1]
    grp = lambda *tail: pl.BlockSpec((None,) + tail, lambda i: (i,) + (0,) * len(tail))
    return pl.pallas_call(
        functools.partial(_ssm_kernel, chunks_p=chunks_p, chunks_s=chunks_s),
        grid=(g,),
        in_specs=[grp(2, 8, p), grp(2, 3, p, t), grp(2, 2, h, p), grp(2, 2, p, h),
                  grp(1, CHUNK_LANES), grp(rows_p, CHUNK_LANES), grp(rows_s, CHUNK_LANES)],
        out_specs=[grp(rows_p, CHUNK_LANES), grp(rows_s, CHUNK_LANES)],
        out_shape=[jax.ShapeDtypeStruct((g, rows_p, CHUNK_LANES), _F32),
                   jax.ShapeDtypeStruct((g, rows_s, CHUNK_LANES), _F32)],
        scratch_shapes=[
            pltpu.VMEM((2, CHUNK_LANES, CHUNK_LANES // W_COL_BLOCKS), _BF16),
            pltpu.VMEM((CHUNK_LANES, 2 * STATE_LANES), _BF16),
            pltpu.VMEM((2 * STATE_LANES, CHUNK_LANES), _BF16),
            pltpu.VMEM((rows_p + rows_s, CHUNK_LANES), _BF16),
            pltpu.VMEM((rows_p + rows_s, 2 * STATE_LANES), _BF16),
        ],
        compiler_params=pltpu.CompilerParams(vmem_limit_bytes=VMEM_LIMIT),
        name="ssm_chunked",
    )(arow, acol, bt, ct, dfl, up, us)


def _tail_kernel(x_ref, yf_ref, gy_ref, mod_ref, wglu_ref, bglu_ref, gf_ref, gss_ref,
                 wout_ref, gmlp_ref, w1_ref, b1_ref, w2_ref, b2_ref, gfin_ref, o_ref, gs_scr):
    sub = pl.program_id(1) % TAIL_GROUP
    nck = gy_ref.shape[1]

    @pl.when(sub == 0)
    def _():
        def per_group(g, carry):
            for hh in range(SSM_GROUP):
                gs_scr[pl.ds(g * SSM_GROUP + hh, nck, stride=ZS_PITCH), :] = \
                    gy_ref[g, :, hh * SSM_CHUNK:(hh + 1) * SSM_CHUNK]
            return carry
        lax.fori_loop(0, N_GROUPS, per_group, 0)

    per_tile = nck // TAIL_GROUP
    slabs = []
    for c in range(per_tile):
        r0 = pl.multiple_of((sub * per_tile + c) * ZS_PITCH, SUBLANES)
        slabs.append(gs_scr[pl.ds(r0, D_SSM), :].T)
    gs = jnp.concatenate(slabs, axis=0).astype(_BF16)

    gate1 = mod_ref[2:3, :]
    shift2 = mod_ref[3:4, :]
    scale2 = mod_ref[4:5, :]
    gate2 = mod_ref[5:6, :]

    ab = jnp.dot(gs, wglu_ref[...], preferred_element_type=_F32) + bglu_ref[...]
    ys = ab[:, :D_SSM] * jax.nn.sigmoid(ab[:, D_SSM:])
    ysn = _rms(ys) * gss_ref[...]
    yfn = _rms(yf_ref[...].astype(_F32)) * gf_ref[...]
    merged = jnp.concatenate([yfn, ysn], axis=-1).astype(_BF16)
    x1 = x_ref[...] + gate1 * jnp.dot(merged, wout_ref[...], preferred_element_type=_F32)

    h2 = (_rms(x1) * gmlp_ref[...] * (1.0 + scale2) + shift2).astype(_BF16)
    ff = jnp.zeros(x1.shape, _F32)
    for c0 in range(0, D_FF, FF_CHUNK):
        a = jnp.dot(h2, w1_ref[:, c0:c0 + FF_CHUNK], preferred_element_type=_F32)
        a = jnp.maximum(a + b1_ref[:, c0:c0 + FF_CHUNK], 0.0)
        a = (a * a).astype(_BF16)
        ff = ff + jnp.dot(a, w2_ref[c0:c0 + FF_CHUNK, :], preferred_element_type=_F32)
    x2 = x1 + gate2 * (ff + b2_ref[...])
    o_ref[...] = _rms(x2) * gfin_ref[...]


def _tail(x, yf, gy, mod, wglu, b_glu, g_f, g_s, wout, g_mlp, w1, b1, w2, b2, g_final):
    b, l, _ = x.shape
    tm = TOKEN_TILE
    nck = TAIL_GROUP * tm // SSM_CHUNK
    blocks = l // (TAIL_GROUP * tm)
    tok = lambda width: pl.BlockSpec((None, tm, width), lambda i, j: (i, j, 0))
    row = lambda a: a.reshape(1, -1)
    single = lambda shape: pl.BlockSpec(shape, lambda *_: (0,) * len(shape),
                                        pipeline_mode=pl.Buffered(1))
    return pl.pallas_call(
        _tail_kernel,
        grid=(b, l // tm),
        in_specs=[
            tok(D_MODEL), tok(D_FOURIER),
            pl.BlockSpec((N_GROUPS, nck, CHUNK_LANES),
                         lambda i, j: (0, i * blocks + j // TAIL_GROUP, 0)),
            pl.BlockSpec((None, N_MOD, D_MODEL), lambda i, j: (i, 0, 0)),
            single((D_SSM, 2 * D_SSM)), single((1, 2 * D_SSM)),
            single((1, D_FOURIER)), single((1, D_SSM)),
            single((D_MODEL, D_MODEL)), single((1, D_MODEL)),
            single((D_MODEL, D_FF)), single((1, D_FF)),
            single((D_FF, D_MODEL)), single((1, D_MODEL)), single((1, D_MODEL)),
        ],
        out_specs=tok(D_MODEL),
        out_shape=jax.ShapeDtypeStruct((b, l, D_MODEL), _F32),
        scratch_shapes=[pltpu.VMEM((nck * ZS_PITCH, LANES), _F32)],
        compiler_params=pltpu.CompilerParams(
            dimension_semantics=("arbitrary", "arbitrary"), vmem_limit_bytes=VMEM_LIMIT),
        name="tail",
    )(x, yf, gy, mod, wglu, row(b_glu), row(g_f), row(g_s), wout, row(g_mlp),
      w1, row(b1), w2, row(b2), row(g_final))


def kernel(x_prompt, x_sample, c_prompt, c_sample, w_ada, b_ada, g_mix_norm, w_in, w_fourier, b_fourier, ssm_a_re, ssm_a_im, ssm_log_dt, ssm_b_re, ssm_b_im, ssm_c_re, ssm_c_im, ssm_d, w_glu, b_glu, g_fourier_out, g_ssm_out, w_out, g_mlp_norm, w_mlp_in, b_mlp_in, w_mlp_out, b_mlp_out, g_final):
    assert w_ada.shape[0] == 1, "single-layer block"
    n_p = c_prompt.shape[0]
    n_s = c_sample.shape[0]
    pad = (-(n_p + n_s)) % 8
    c_all = jnp.concatenate(
        [c_prompt, c_sample, jnp.zeros((pad, D_MODEL), c_prompt.dtype)], axis=0)
    mod = _adaln(c_all, w_ada[0], b_ada[0]).reshape(-1, N_MOD, D_MODEL)

    wf_b = w_in[0][:, :D_FOURIER].astype(_BF16)
    wst_b = w_in[0][:, D_FOURIER:].T.astype(_BF16)
    wcomb = _fold_fourier_weight(w_fourier[0])
    wglu_b = w_glu[0].astype(_BF16)
    wout_b = w_out[0].astype(_BF16)
    w1_b = w_mlp_in[0].astype(_BF16)
    w2_b = w_mlp_out[0].astype(_BF16)
    mod_p = mod[:n_p]
    mod_s = mod[n_p:n_p + n_s]

    zf_p, u_p = _inproj(x_prompt, mod_p, g_mix_norm[0], wf_b, wst_b)
    zf_s, u_s = _inproj(x_sample, mod_s, g_mix_norm[0], wf_b, wst_b)
    gy_p, gy_s = _ssm_mixer(u_p, u_s, x_prompt.shape[1] // SSM_CHUNK,
                            x_sample.shape[1] // SSM_CHUNK, ssm_a_re[0], ssm_a_im[0],
                            ssm_log_dt[0], ssm_b_re[0], ssm_b_im[0], ssm_c_re[0],
                            ssm_c_im[0], ssm_d[0])

    def finish(x, m, zf, gy):
        yf = _fourier_mixer(zf, wcomb, b_fourier[0])
        return _tail(x, yf, gy, m, wglu_b, b_glu[0], g_fourier_out[0], g_ssm_out[0],
                     wout_b, g_mlp_norm[0], w1_b, b_mlp_in[0], w2_b, b_mlp_out[0], g_final)

    return (finish(x_prompt, mod_p, zf_p, gy_p), finish(x_sample, mod_s, zf_s, gy_s))
```

```python
import functools
import math

import numpy as np
import jax
import jax.numpy as jnp
from jax import lax
from jax.experimental import pallas as pl
from jax.experimental.pallas import tpu as pltpu

D_MODEL = 1024
D_FOURIER = 512
D_SSM = 512
N_HEADS = 4
HEAD_DIM = 128
SSM_GROUP = 16
N_GROUPS = 32
SSM_STATE = 64
D_FF = 4096
N_MOD = 6
EPS = 1e-6
A_RE_MAX = -1e-4

SSM_CHUNK = 128
CHUNK_LANES = SSM_CHUNK * SSM_GROUP
STATE_LANES = 2 * SSM_STATE
W_COL_BLOCKS = 4
DFT_N1 = 128
DFT_GROUP_ROWS = 128
DFT1_COLS = 16
LANES = 128
SUBLANES = 8
ROW_PAD = 8
ZS_PITCH = D_SSM + ROW_PAD
INPROJ_TILE = 2048
INPROJ_SUB = 256
TOKEN_TILE = 512
TAIL_GROUP = 2
FF_CHUNK = 1024
VMEM_LIMIT = 56 * 1024 * 1024

_BF16 = jnp.bfloat16
_F32 = jnp.float32


def _rms(v):
    return v * lax.rsqrt(jnp.mean(v * v, axis=-1, keepdims=True) + EPS)


def _const_spec(shape):
    nd = len(shape)
    return pl.BlockSpec(shape, lambda *_: (0,) * nd)


def _adaln_kernel(c_ref, w_ref, b_ref, o_ref):
    c = c_ref[...]
    s = c * jax.nn.sigmoid(c)
    o_ref[...] = jnp.dot(s, w_ref[...], preferred_element_type=_F32,
                         precision=lax.Precision.HIGHEST) + b_ref[...]


def _adaln(c_all, w_ada, b_ada):
    rows = c_all.shape[0]
    return pl.pallas_call(
        _adaln_kernel,
        grid=(N_MOD,),
        in_specs=[
            _const_spec((rows, D_MODEL)),
            pl.BlockSpec((D_MODEL, D_MODEL), lambda j: (0, j)),
            pl.BlockSpec((1, D_MODEL), lambda j: (0, j)),
        ],
        out_specs=pl.BlockSpec((rows, D_MODEL), lambda j: (0, j)),
        out_shape=jax.ShapeDtypeStruct((rows, N_MOD * D_MODEL), _F32),
        name="adaln",
    )(c_all, w_ada, b_ada.reshape(1, -1))


def _fold_kernel(cs_ref, wf_ref, o_ref):
    o_ref[...] = jnp.dot(cs_ref[...], wf_ref[...], preferred_element_type=_F32,
                         precision=lax.Precision.HIGHEST).astype(o_ref.dtype)


@functools.lru_cache(maxsize=None)
def _channel_dft_np():
    k = np.arange(D_FOURIER)
    ang = 2.0 * np.pi * ((k[:, None] * k[None, :]) % D_FOURIER) / D_FOURIER
    scale = 1.0 / math.sqrt(D_FOURIER)
    return np.concatenate([np.cos(ang), np.sin(ang)], axis=0).astype(np.float32) * scale


def _fold_fourier_weight(w_fourier):
    wblk = jnp.zeros((D_FOURIER, D_FOURIER), _F32)
    for h in range(N_HEADS):
        sl = slice(h * HEAD_DIM, (h + 1) * HEAD_DIM)
        wblk = wblk.at[sl, sl].set(w_fourier[h])
    cs = jnp.asarray(_channel_dft_np())
    return pl.pallas_call(
        _fold_kernel,
        out_shape=jax.ShapeDtypeStruct((2 * D_FOURIER, D_FOURIER), _BF16),
        name="fold_fourier_weight",
    )(cs, wblk)


def _inproj_kernel(x_ref, mod_ref, g_ref, wf_ref, wst_ref, zf_ref, u_ref, zs_scr):
    tm = x_ref.shape[0]
    nck = tm // SSM_CHUNK
    for s0 in range(0, tm, INPROJ_SUB):
        x = x_ref[s0:s0 + INPROJ_SUB, :]
        h = (_rms(x) * g_ref[...] * (1.0 + mod_ref[1:2, :]) + mod_ref[0:1, :]).astype(_BF16)
        zf_ref[s0:s0 + INPROJ_SUB, :] = jnp.dot(
            h, wf_ref[...], preferred_element_type=_F32).astype(_BF16)
        zst = lax.dot_general(wst_ref[...], h, (((1,), (1,)), ((), ())),
                              preferred_element_type=_F32)
        for c in range(s0 // SSM_CHUNK, (s0 + INPROJ_SUB) // SSM_CHUNK):
            zs_scr[c * ZS_PITCH:c * ZS_PITCH + D_SSM, :] = \
                zst[:, c * SSM_CHUNK - s0:(c + 1) * SSM_CHUNK - s0]

    for g in range(N_GROUPS):
        for hh in range(SSM_GROUP):
            u_ref[g, :, hh * SSM_CHUNK:(hh + 1) * SSM_CHUNK] = \
                zs_scr[pl.ds(g * SSM_GROUP + hh, nck, stride=ZS_PITCH), :].astype(_BF16)


def _inproj(x, mod, g_mix, wf_bf16, wst_bf16):
    b, l, _ = x.shape
    tm = INPROJ_TILE
    nck = tm // SSM_CHUNK
    steps = l // tm
    return pl.pallas_call(
        _inproj_kernel,
        grid=(b, steps),
        in_specs=[
            pl.BlockSpec((None, tm, D_MODEL), lambda i, j: (i, j, 0)),
            pl.BlockSpec((None, N_MOD, D_MODEL), lambda i, j: (i, 0, 0)),
            _const_spec((1, D_MODEL)),
            _const_spec((D_MODEL, D_FOURIER)),
            _const_spec((D_SSM, D_MODEL)),
        ],
        out_specs=[
            pl.BlockSpec((None, tm, D_FOURIER), lambda i, j: (i, j, 0)),
            pl.BlockSpec((N_GROUPS, nck, CHUNK_LANES), lambda i, j: (0, i * steps + j, 0)),
        ],
        out_shape=[
            jax.ShapeDtypeStruct((b, l, D_FOURIER), _BF16),
            jax.ShapeDtypeStruct((N_GROUPS, b * l // SSM_CHUNK, CHUNK_LANES), _BF16),
        ],
        scratch_shapes=[pltpu.VMEM((nck * ZS_PITCH, LANES), _F32)],
        compiler_params=pltpu.CompilerParams(vmem_limit_bytes=VMEM_LIMIT),
        name="inproj",
    )(x, mod, g_mix.reshape(1, -1), wf_bf16, wst_bf16)


@functools.lru_cache(maxsize=None)
def _dft_tables_np(n):
    n1 = DFT_N1
    n2 = n // n1
    i1 = np.arange(n1)
    ang1 = 2.0 * np.pi * ((i1[:, None] * i1[None, :]) % n1) / n1
    f1 = np.concatenate([np.cos(ang1), -np.sin(ang1)], axis=0).astype(np.float32)
    i2 = np.arange(n2)
    m = (i2[None, None, :] * (i1[:, None, None] + n1 * i2[None, :, None])) % n
    ang = 2.0 * np.pi * m / n
    gr = np.cos(ang) / math.sqrt(n)
    gi = -np.sin(ang) / math.sqrt(n)
    gsz = max(1, DFT_GROUP_ROWS // n2)
    eye = np.eye(gsz)
    bd = lambda m: np.einsum("qjkn,jl->qjkln", m.reshape(n1 // gsz, gsz, n2, n2),
                             eye).reshape(n1 // gsz, gsz * n2, gsz * n2)
    gr, gi = bd(gr), bd(gi)
    g = np.concatenate([np.concatenate([gr, -gi], axis=2),
                        np.concatenate([gi, gr], axis=2)], axis=1).astype(np.float32)
    return f1, g


def _dft1_kernel(f_ref, x_ref, a_ref):
    xt = jnp.swapaxes(x_ref[...], 0, 1)
    for q in range(DFT1_COLS):
        a_ref[:, q * D_FOURIER:(q + 1) * D_FOURIER] = \
            jnp.dot(f_ref[...], xt[q],
                    preferred_element_type=_F32).astype(_BF16)


def _dft_stage1(zf, f1):
    b, l, _ = zf.shape
    n1 = DFT_N1
    n2 = l // n1
    cw = DFT1_COLS * D_FOURIER
    return pl.pallas_call(
        _dft1_kernel,
        grid=(b, n2 // DFT1_COLS),
        in_specs=[
            _const_spec((2 * n1, n1)),
            pl.BlockSpec((None, n1, DFT1_COLS, D_FOURIER), lambda i, j: (i, 0, j, 0)),
        ],
        out_specs=pl.BlockSpec((None, 2 * n1, cw), lambda i, j: (i, 0, j)),
        out_shape=jax.ShapeDtypeStruct((b, 2 * n1, n2 * D_FOURIER), _BF16),
        name="dft_stage1",
    )(f1, zf.reshape(b, n1, n2, D_FOURIER))


def _dft2_kernel(a_ref, g_ref, wc_ref, bf_ref, o_ref, *, kb, n2):
    gsz = kb // g_ref.shape[0]
    rows = gsz * n2
    xr, xi = [], []
    for q in range(kb // gsz):
        ks = slice(q * gsz, (q + 1) * gsz)
        aa = jnp.concatenate([a_ref[0, ks].reshape(rows, D_FOURIER),
                              a_ref[1, ks].reshape(rows, D_FOURIER)], axis=0)
        x = jnp.dot(g_ref[q], aa, preferred_element_type=_F32)
        xr.append(x[:rows].astype(_BF16))
        xi.append(x[rows:].astype(_BF16))
    xr = jnp.concatenate(xr, axis=0)
    xi = jnp.concatenate(xi, axis=0)
    y = jnp.dot(xr, wc_ref[:D_FOURIER, :], preferred_element_type=_F32)
    y = y + jnp.dot(xi, wc_ref[D_FOURIER:, :], preferred_element_type=_F32)
    y = y + bf_ref[...]
    o_ref[...] = y.reshape(kb, n2, D_FOURIER).astype(o_ref.dtype)


def _dft_stage2(a, g, wcomb, b_fourier):
    b = a.shape[0]
    n1 = DFT_N1
    n2 = a.shape[2] // D_FOURIER
    gsz = n1 // g.shape[0]
    kb = max(8, 1024 // n2)
    a5 = a.reshape(b, 2, n1, n2, D_FOURIER)
    return pl.pallas_call(
        functools.partial(_dft2_kernel, kb=kb, n2=n2),
        grid=(b, n1 // kb),
        in_specs=[
            pl.BlockSpec((None, 2, kb, n2, D_FOURIER), lambda i, j: (i, 0, j, 0, 0)),
            pl.BlockSpec((kb // gsz, 2 * gsz * n2, 2 * gsz * n2), lambda i, j: (j, 0, 0)),
            _const_spec((2 * D_FOURIER, D_FOURIER)),
            _const_spec((1, D_FOURIER)),
        ],
        out_specs=pl.BlockSpec((None, kb, n2, D_FOURIER), lambda i, j: (i, j, 0, 0)),
        out_shape=jax.ShapeDtypeStruct((b, n1, n2, D_FOURIER), _BF16),
        name="dft_stage2",
    )(a5, g, wcomb, b_fourier.reshape(1, -1))


def _fourier_mixer(zf, wcomb, b_fourier):
    b, l, _ = zf.shape
    f1_np, g_np = _dft_tables_np(l)
    f1 = jnp.asarray(f1_np).astype(_BF16)
    g = jnp.asarray(g_np).astype(_BF16)
    a = _dft_stage1(zf, f1)
    y = _dft_stage2(a, g, wcomb, b_fourier)
    return jnp.swapaxes(y, 1, 2).reshape(b, l, D_FOURIER)


def _cpow(k, zr, zi):
    mag = jnp.exp(k * zr)
    return mag * jnp.cos(k * zi), mag * jnp.sin(k * zi)


_GELU_C1 = -2.0 * math.sqrt(2.0 / math.pi) * math.log2(math.e)
_GELU_C2 = 0.044715 * _GELU_C1


def _gelu_tanh(x):
    return x / (1.0 + jnp.exp2(x * (_GELU_C1 + _GELU_C2 * (x * x))))


def _dot3(a, b):
    a_hi = a.astype(_BF16)
    b_hi = b.astype(_BF16)
    a_lo = (a - a_hi.astype(_F32)).astype(_BF16)
    b_lo = (b - b_hi.astype(_F32)).astype(_BF16)
    dot = functools.partial(jnp.dot, preferred_element_type=_F32)
    return dot(a_hi, b_hi) + (dot(a_hi, b_lo) + dot(a_lo, b_hi))


def _discretise(a_re, a_im, log_dt):
    lam_re = jnp.minimum(a_re, A_RE_MAX)
    dt = jnp.exp(log_dt)
    return lam_re, a_im, lam_re * dt, a_im * dt


def _ssm_kernel(arow_ref, acol_ref, bt_ref, ct_ref, d_ref, up_ref, us_ref,
                op_ref, os_ref, w_scr, bm_scr, e_scr, ub_scr, hc_scr, *, chunks_p, chunks_s):
    t = SSM_CHUNK
    p = SSM_STATE
    lane_k = lax.broadcasted_iota(jnp.int32, (p, t), 1).astype(_F32)
    sub_k = lax.broadcasted_iota(jnp.int32, (t, p), 0).astype(_F32)
    lane0 = lax.broadcasted_iota(jnp.int32, (p, t), 1) == 0

    zrow, bb = [], []
    for d in range(2):
        lam_re, lam_im, zr, zi = _discretise(arow_ref[d, 0:1, :], arow_ref[d, 1:2, :],
                                             arow_ref[d, 2:3, :])
        th = jnp.tanh(0.5 * zr)
        em1 = 2.0 * th / (1.0 - th)
        e_re = em1 * jnp.cos(zi) - 2.0 * jnp.square(jnp.sin(0.5 * zi))
        e_im = (em1 + 1.0) * jnp.sin(zi)
        den = lam_re * lam_re + lam_im * lam_im
        q_re = (e_re * lam_re + e_im * lam_im) / den
        q_im = (e_im * lam_re - e_re * lam_im) / den
        b_re = bt_ref[d, 0]
        b_im = bt_ref[d, 1]
        bb.append((q_re * b_re - q_im * b_im, q_re * b_im + q_im * b_re))
        zrow.append((zr, zi))

    def c_times(d, pw_re, pw_im):
        re, im = [], []
        for h in range(SSM_GROUP):
            c_re = jnp.broadcast_to(ct_ref[d, 0][:, h:h + 1], (p, t))
            c_im = jnp.broadcast_to(ct_ref[d, 1][:, h:h + 1], (p, t))
            re.append(c_re * pw_re - c_im * pw_im)
            im.append(c_re * pw_im + c_im * pw_re)
        return jnp.concatenate(re, axis=1), jnp.concatenate(im, axis=1)

    _, _, zr0, zi0 = _discretise(acol_ref[0, 0], acol_ref[0, 1], acol_ref[0, 2])
    _, _, zr1, zi1 = _discretise(acol_ref[1, 0], acol_ref[1, 1], acol_ref[1, 2])
    caf_re, caf_im = c_times(0, *_cpow(lane_k, zr0, zi0))
    cab_re, cab_im = c_times(1, *_cpow(float(t) - lane_k, zr1, zi1))
    one = jnp.where(lane0, 1.0, 0.0)
    cb0_re, cb0_im = c_times(1, one, jnp.zeros_like(one))

    e_scr[0 * p:1 * p, :] = caf_re.astype(_BF16)
    e_scr[1 * p:2 * p, :] = (-caf_im).astype(_BF16)
    e_scr[2 * p:3 * p, :] = cab_re.astype(_BF16)
    e_scr[3 * p:4 * p, :] = (-cab_im).astype(_BF16)

    (bbf_re, bbf_im), (bbb_re, bbb_im) = bb
    lhs_f = jnp.concatenate([bbf_re, -bbf_im, bbb_re, -bbb_im], axis=1)
    rhs_f = jnp.concatenate([caf_re, caf_im, cb0_re, cb0_im], axis=0)
    kf = _dot3(lhs_f, rhs_f)
    lhs_b = jnp.concatenate([bbb_re, -bbb_im], axis=1)
    rhs_b = jnp.concatenate([cab_re, cab_im], axis=0)
    kb = _dot3(lhs_b, rhs_b)

    pf_re, pf_im = _cpow(float(t - 1) - sub_k, *zrow[0])
    pb_re, pb_im = _cpow(sub_k, *zrow[1])
    for hp in range(SSM_GROUP):
        f_re, f_im = bbf_re[hp:hp + 1, :], bbf_im[hp:hp + 1, :]
        g_re, g_im = bbb_re[hp:hp + 1, :], bbb_im[hp:hp + 1, :]
        blk = jnp.concatenate([pf_re * f_re - pf_im * f_im, pf_re * f_im + pf_im * f_re,
                               pb_re * g_re - pb_im * g_im, pb_re * g_im + pb_im * g_re],
                              axis=1)
        bm_scr[hp * t:(hp + 1) * t, :] = blk.astype(_BF16)

    sl = STATE_LANES
    rows_p = up_ref.shape[0]
    ub_scr[:rows_p, :] = up_ref[...]
    ub_scr[rows_p:, :] = us_ref[...]
    s_all = jnp.dot(ub_scr[...], bm_scr[...], preferred_element_type=_F32)

    def cmul(x, ar, ai):
        a1 = jnp.concatenate([ar, ar], axis=1)
        a2 = jnp.concatenate([-ai, ai], axis=1)
        return x * a1 + pltpu.roll(x, SSM_STATE, 1) * a2

    n_levels = max(chunks_p, chunks_s).bit_length() - 1
    chunk_pow = []
    for d in range(2):
        ar, ai = _cpow(float(t), *zrow[d])
        levels = [(ar, ai)]
        for _ in range(n_levels - 1):
            ar, ai = ar * ar - ai * ai, 2.0 * ar * ai
            levels.append((ar, ai))
        chunk_pow.append(levels)

    def carry_scan(s, n_chunks, r0):
        rows = s.shape[0]
        pos = lax.broadcasted_iota(jnp.int32, (rows, sl), 0) & (n_chunks - 1)

        def shift_rows(x, n, down):
            if down:
                return pltpu.roll(x, n, 0) * (pos >= n).astype(_F32)
            return pltpu.roll(x, rows - n, 0) * (pos < n_chunks - n).astype(_F32)

        xf = s[:, :sl]
        xb = s[:, sl:]
        dist, level = 1, 0
        while dist < n_chunks:
            xf = xf + cmul(shift_rows(xf, dist, True), *chunk_pow[0][level])
            xb = xb + cmul(shift_rows(xb, dist, False), *chunk_pow[1][level])
            dist *= 2
            level += 1
        hf = cmul(shift_rows(xf, 1, True), *_cpow(1.0, *zrow[0]))
        hc_scr[r0:r0 + rows, :sl] = hf.astype(_BF16)
        hc_scr[r0:r0 + rows, sl:] = shift_rows(xb, 1, False).astype(_BF16)

    carry_scan(s_all[:rows_p], chunks_p, 0)
    carry_scan(s_all[rows_p:], chunks_s, rows_p)

    hb = SSM_GROUP // W_COL_BLOCKS
    for nb in range(W_COL_BLOCKS):
        slot = nb % 2
        for hp in range(SSM_GROUP):
            for hl in range(hb):
                h = nb * hb + hl
                lag = jnp.concatenate([kf[hp:hp + 1, h * t:(h + 1) * t],
                                       kb[hp:hp + 1, h * t:(h + 1) * t]], axis=1)
                skew = pltpu.roll(jnp.broadcast_to(lag, (t, 2 * t)), 0, 1,
                                  stride=1, stride_axis=0)
                w_scr[slot, hp * t:(hp + 1) * t, hl * t:(hl + 1) * t] = \
                    skew[:, :t].astype(_BF16)
        c0, c1 = nb * hb * t, (nb + 1) * hb * t
        y = jnp.dot(ub_scr[...], w_scr[slot], preferred_element_type=_F32)
        y = y + jnp.dot(hc_scr[...], e_scr[:, c0:c1], preferred_element_type=_F32)
        yp = y[:rows_p] + d_ref[:, c0:c1] * up_ref[:, c0:c1].astype(_F32)
        ys = y[rows_p:] + d_ref[:, c0:c1] * us_ref[:, c0:c1].astype(_F32)
        op_ref[:, c0:c1] = _gelu_tanh(yp)
        os_ref[:, c0:c1] = _gelu_tanh(ys)


def _ssm_mixer(up, us, chunks_p, chunks_s, a_re, a_im, log_dt, b_re, b_im, c_re, c_im, d_skip):
    g, p, h, t = N_GROUPS, SSM_STATE, SSM_GROUP, SSM_CHUNK
    ldt = jnp.broadcast_to(log_dt[:, :, None], (2, g, p))
    arow = jnp.stack([a_re, a_im, ldt] + [jnp.zeros_like(a_re)] * 5, axis=2)
    arow = jnp.transpose(arow, (1, 0, 2, 3)).astype(_F32)
    acol = jnp.stack([a_re, a_im, ldt], axis=2)
    acol = jnp.transpose(acol, (1, 0, 2, 3)).astype(_F32)
    acol = jnp.broadcast_to(acol[..., None], (g, 2, 3, p, t))
    bt = jnp.transpose(jnp.stack([b_re, b_im], axis=2), (1, 0, 2, 4, 3)).astype(_F32)
    ct = jnp.transpose(jnp.stack([c_re, c_im], axis=2), (1, 0, 2, 4, 3)).astype(_F32)
    dfl = jnp.repeat(d_skip.astype(_F32).reshape(g, 1, h), t, axis=2)

    rows_p, rows_s = up.shape[1], us.shape[1]
    grp = lambda *tail: pl.BlockSpec((None,) + tail, lambda i: (i,) + (0,) * len(tail))
    return pl.pallas_call(
        functools.partial(_ssm_kernel, chunks_p=chunks_p, chunks_s=chunks_s),
        grid=(g,),
        in_specs=[grp(2, 8, p), grp(2, 3, p, t), grp(2, 2, h, p), grp(2, 2, p, h),
                  grp(1, CHUNK_LANES), grp(rows_p, CHUNK_LANES), grp(rows_s, CHUNK_LANES)],
        out_specs=[grp(rows_p, CHUNK_LANES), grp(rows_s, CHUNK_LANES)],
        out_shape=[jax.ShapeDtypeStruct((g, rows_p, CHUNK_LANES), _F32),
                   jax.ShapeDtypeStruct((g, rows_s, CHUNK_LANES), _F32)],
        scratch_shapes=[
            pltpu.VMEM((2, CHUNK_LANES, CHUNK_LANES // W_COL_BLOCKS), _BF16),
            pltpu.VMEM((CHUNK_LANES, 2 * STATE_LANES), _BF16),
            pltpu.VMEM((2 * STATE_LANES, CHUNK_LANES), _BF16),
            pltpu.VMEM((rows_p + rows_s, CHUNK_LANES), _BF16),
            pltpu.VMEM((rows_p + rows_s, 2 * STATE_LANES), _BF16),
        ],
        compiler_params=pltpu.CompilerParams(vmem_limit_bytes=VMEM_LIMIT),
        name="ssm_chunked",
    )(arow, acol, bt, ct, dfl, up, us)


def _tail_kernel(x_ref, yf_ref, gy_ref, mod_ref, wglu_ref, bglu_ref, gf_ref, gss_ref,
                 wout_ref, gmlp_ref, w1_ref, b1_ref, w2_ref, b2_ref, gfin_ref, o_ref, gs_scr):
    sub = pl.program_id(1) % TAIL_GROUP
    nck = gy_ref.shape[1]

    @pl.when(sub == 0)
    def _():
        def per_group(g, carry):
            for hh in range(SSM_GROUP):
                gs_scr[pl.ds(g * SSM_GROUP + hh, nck, stride=ZS_PITCH), :] = \
                    gy_ref[g, :, hh * SSM_CHUNK:(hh + 1) * SSM_CHUNK]
            return carry
        lax.fori_loop(0, N_GROUPS, per_group, 0)

    per_tile = nck // TAIL_GROUP
    slabs = []
    for c in range(per_tile):
        r0 = pl.multiple_of((sub * per_tile + c) * ZS_PITCH, SUBLANES)
        slabs.append(gs_scr[pl.ds(r0, D_SSM), :].T)
    gs = jnp.concatenate(slabs, axis=0).astype(_BF16)

    gate1 = mod_ref[2:3, :]
    shift2 = mod_ref[3:4, :]
    scale2 = mod_ref[4:5, :]
    gate2 = mod_ref[5:6, :]

    ab = jnp.dot(gs, wglu_ref[...], preferred_element_type=_F32) + bglu_ref[...]
    ys = ab[:, :D_SSM] * jax.nn.sigmoid(ab[:, D_SSM:])
    ysn = _rms(ys) * gss_ref[...]
    yfn = _rms(yf_ref[...].astype(_F32)) * gf_ref[...]
    merged = jnp.concatenate([yfn, ysn], axis=-1).astype(_BF16)
    x1 = x_ref[...] + gate1 * jnp.dot(merged, wout_ref[...], preferred_element_type=_F32)

    h2 = (_rms(x1) * gmlp_ref[...] * (1.0 + scale2) + shift2).astype(_BF16)
    ff = jnp.zeros(x1.shape, _F32)
    for c0 in range(0, D_FF, FF_CHUNK):
        a = jnp.dot(h2, w1_ref[:, c0:c0 + FF_CHUNK], preferred_element_type=_F32)
        a = jnp.maximum(a + b1_ref[:, c0:c0 + FF_CHUNK], 0.0)
        a = (a * a).astype(_BF16)
        ff = ff + jnp.dot(a, w2_ref[c0:c0 + FF_CHUNK, :], preferred_element_type=_F32)
    x2 = x1 + gate2 * (ff + b2_ref[...])
    o_ref[...] = _rms(x2) * gfin_ref[...]


def _tail(x, yf, gy, mod, wglu, b_glu, g_f, g_s, wout, g_mlp, w1, b1, w2, b2, g_final):
    b, l, _ = x.shape
    tm = TOKEN_TILE
    nck = TAIL_GROUP * tm // SSM_CHUNK
    blocks = l // (TAIL_GROUP * tm)
    tok = lambda width: pl.BlockSpec((None, tm, width), lambda i, j: (i, j, 0))
    row = lambda a: a.reshape(1, -1)
    single = lambda shape: pl.BlockSpec(shape, lambda *_: (0,) * len(shape),
                                        pipeline_mode=pl.Buffered(1))
    return pl.pallas_call(
        _tail_kernel,
        grid=(b, l // tm),
        in_specs=[
            tok(D_MODEL), tok(D_FOURIER),
            pl.BlockSpec((N_GROUPS, nck, CHUNK_LANES),
                         lambda i, j: (0, i * blocks + j // TAIL_GROUP, 0)),
            pl.BlockSpec((None, N_MOD, D_MODEL), lambda i, j: (i, 0, 0)),
            single((D_SSM, 2 * D_SSM)), single((1, 2 * D_SSM)),
            single((1, D_FOURIER)), single((1, D_SSM)),
            single((D_MODEL, D_MODEL)), single((1, D_MODEL)),
            single((D_MODEL, D_FF)), single((1, D_FF)),
            single((D_FF, D_MODEL)), single((1, D_MODEL)), single((1, D_MODEL)),
        ],
        out_specs=tok(D_MODEL),
        out_shape=jax.ShapeDtypeStruct((b, l, D_MODEL), _F32),
        scratch_shapes=[pltpu.VMEM((nck * ZS_PITCH, LANES), _F32)],
        compiler_params=pltpu.CompilerParams(
            dimension_semantics=("arbitrary", "arbitrary"), vmem_limit_bytes=VMEM_LIMIT),
        name="tail",
    )(x, yf, gy, mod, wglu, row(b_glu), row(g_f), row(g_s), wout, row(g_mlp),
      w1, row(b1), w2, row(b2), row(g_final))


def kernel(x_prompt, x_sample, c_prompt, c_sample, w_ada, b_ada, g_mix_norm, w_in, w_fourier, b_fourier, ssm_a_re, ssm_a_im, ssm_log_dt, ssm_b_re, ssm_b_im, ssm_c_re, ssm_c_im, ssm_d, w_glu, b_glu, g_fourier_out, g_ssm_out, w_out, g_mlp_norm, w_mlp_in, b_mlp_in, w_mlp_out, b_mlp_out, g_final):
    assert w_ada.shape[0] == 1, "single-layer block"
    n_p = c_prompt.shape[0]
    n_s = c_sample.shape[0]
    pad = (-(n_p + n_s)) % 8
    c_all = jnp.concatenate(
        [c_prompt, c_sample, jnp.zeros((pad, D_MODEL), c_prompt.dtype)], axis=0)
    mod = _adaln(c_all, w_ada[0], b_ada[0]).reshape(-1, N_MOD, D_MODEL)

    wf_b = w_in[0][:, :D_FOURIER].astype(_BF16)
    wst_b = w_in[0][:, D_FOURIER:].T.astype(_BF16)
    wcomb = _fold_fourier_weight(w_fourier[0])
    wglu_b = w_glu[0].astype(_BF16)
    wout_b = w_out[0].astype(_BF16)
    w1_b = w_mlp_in[0].astype(_BF16)
    w2_b = w_mlp_out[0].astype(_BF16)
    mod_p = mod[:n_p]
    mod_s = mod[n_p:n_p + n_s]

    zf_p, u_p = _inproj(x_prompt, mod_p, g_mix_norm[0], wf_b, wst_b)
    zf_s, u_s = _inproj(x_sample, mod_s, g_mix_norm[0], wf_b, wst_b)
    gy_p, gy_s = _ssm_mixer(u_p, u_s, x_prompt.shape[1] // SSM_CHUNK,
                            x_sample.shape[1] // SSM_CHUNK, ssm_a_re[0], ssm_a_im[0],
                            ssm_log_dt[0], ssm_b_re[0], ssm_b_im[0], ssm_c_re[0],
                            ssm_c_im[0], ssm_d[0])

    def finish(x, m, zf, gy):
        yf = _fourier_mixer(zf, wcomb, b_fourier[0])
        return _tail(x, yf, gy, m, wglu_b, b_glu[0], g_fourier_out[0], g_ssm_out[0],
                     wout_b, g_mlp_norm[0], w1_b, b_mlp_in[0], w2_b, b_mlp_out[0], g_final)

    return (finish(x_prompt, mod_p, zf_p, gy_p), finish(x_sample, mod_s, zf_s, gy_s))
```

```python
import functools
import math

import numpy as np
import jax
import jax.numpy as jnp
from jax import lax
from jax.experimental import pallas as pl
from jax.experimental.pallas import tpu as pltpu

D_MODEL = 1024
D_FOURIER = 512
D_SSM = 512
N_HEADS = 4
HEAD_DIM = 128
SSM_GROUP = 16
N_GROUPS = 32
SSM_STATE = 64
D_FF = 4096
N_MOD = 6
EPS = 1e-6
A_RE_MAX = -1e-4

SSM_CHUNK = 128
CHUNK_LANES = SSM_CHUNK * SSM_GROUP
STATE_LANES = 2 * SSM_STATE
W_COL_BLOCKS = 4
DFT_N1 = 128
DFT_GROUP_ROWS = 128
DFT1_COLS = 16
LANES = 128
SUBLANES = 8
ROW_PAD = 8
ZS_PITCH = D_SSM + ROW_PAD
INPROJ_TILE = 2048
INPROJ_SUB = 256
TOKEN_TILE = 1024
TAIL_GROUP = 1
FF_CHUNK = 1024
VMEM_LIMIT = 56 * 1024 * 1024

_BF16 = jnp.bfloat16
_F32 = jnp.float32


def _rms(v):
    return v * lax.rsqrt(jnp.mean(v * v, axis=-1, keepdims=True) + EPS)


def _const_spec(shape):
    nd = len(shape)
    return pl.BlockSpec(shape, lambda *_: (0,) * nd)


def _adaln_kernel(c_ref, w_ref, b_ref, o_ref):
    c = c_ref[...]
    s = c * jax.nn.sigmoid(c)
    o_ref[...] = jnp.dot(s, w_ref[...], preferred_element_type=_F32,
                         precision=lax.Precision.HIGHEST) + b_ref[...]


def _adaln(c_all, w_ada, b_ada):
    rows = c_all.shape[0]
    return pl.pallas_call(
        _adaln_kernel,
        grid=(N_MOD,),
        in_specs=[
            _const_spec((rows, D_MODEL)),
            pl.BlockSpec((D_MODEL, D_MODEL), lambda j: (0, j)),
            pl.BlockSpec((1, D_MODEL), lambda j: (0, j)),
        ],
        out_specs=pl.BlockSpec((rows, D_MODEL), lambda j: (0, j)),
        out_shape=jax.ShapeDtypeStruct((rows, N_MOD * D_MODEL), _F32),
        name="adaln",
    )(c_all, w_ada, b_ada.reshape(1, -1))


def _fold_kernel(cs_ref, wf_ref, o_ref):
    o_ref[...] = jnp.dot(cs_ref[...], wf_ref[...], preferred_element_type=_F32,
                         precision=lax.Precision.HIGHEST).astype(o_ref.dtype)


@functools.lru_cache(maxsize=None)
def _channel_dft_np():
    k = np.arange(D_FOURIER)
    ang = 2.0 * np.pi * ((k[:, None] * k[None, :]) % D_FOURIER) / D_FOURIER
    scale = 1.0 / math.sqrt(D_FOURIER)
    return np.concatenate([np.cos(ang), np.sin(ang)], axis=0).astype(np.float32) * scale


def _fold_fourier_weight(w_fourier):
    wblk = jnp.zeros((D_FOURIER, D_FOURIER), _F32)
    for h in range(N_HEADS):
        sl = slice(h * HEAD_DIM, (h + 1) * HEAD_DIM)
        wblk = wblk.at[sl, sl].set(w_fourier[h])
    cs = jnp.asarray(_channel_dft_np())
    return pl.pallas_call(
        _fold_kernel,
        out_shape=jax.ShapeDtypeStruct((2 * D_FOURIER, D_FOURIER), _BF16),
        name="fold_fourier_weight",
    )(cs, wblk)


def _inproj_kernel(x_ref, mod_ref, g_ref, wf_ref, wst_ref, zf_ref, u_ref, zs_scr):
    tm = x_ref.shape[0]
    nck = tm // SSM_CHUNK
    for s0 in range(0, tm, INPROJ_SUB):
        x = x_ref[s0:s0 + INPROJ_SUB, :]
        h = (_rms(x) * g_ref[...] * (1.0 + mod_ref[1:2, :]) + mod_ref[0:1, :]).astype(_BF16)
        zf_ref[s0:s0 + INPROJ_SUB, :] = jnp.dot(
            h, wf_ref[...], preferred_element_type=_F32).astype(_BF16)
        zst = lax.dot_general(wst_ref[...], h, (((1,), (1,)), ((), ())),
                              preferred_element_type=_F32)
        for c in range(s0 // SSM_CHUNK, (s0 + INPROJ_SUB) // SSM_CHUNK):
            zs_scr[c * ZS_PITCH:c * ZS_PITCH + D_SSM, :] = \
                zst[:, c * SSM_CHUNK - s0:(c + 1) * SSM_CHUNK - s0]

    for g in range(N_GROUPS):
        for hh in range(SSM_GROUP):
            u_ref[g, :, hh * SSM_CHUNK:(hh + 1) * SSM_CHUNK] = \
                zs_scr[pl.ds(g * SSM_GROUP + hh, nck, stride=ZS_PITCH), :].astype(_BF16)


def _inproj(x, mod, g_mix, wf_bf16, wst_bf16):
    b, l, _ = x.shape
    tm = INPROJ_TILE
    nck = tm // SSM_CHUNK
    steps = l // tm
    return pl.pallas_call(
        _inproj_kernel,
        grid=(b, steps),
        in_specs=[
            pl.BlockSpec((None, tm, D_MODEL), lambda i, j: (i, j, 0)),
            pl.BlockSpec((None, N_MOD, D_MODEL), lambda i, j: (i, 0, 0)),
            _const_spec((1, D_MODEL)),
            _const_spec((D_MODEL, D_FOURIER)),
            _const_spec((D_SSM, D_MODEL)),
        ],
        out_specs=[
            pl.BlockSpec((None, tm, D_FOURIER), lambda i, j: (i, j, 0)),
            pl.BlockSpec((N_GROUPS, nck, CHUNK_LANES), lambda i, j: (0, i * steps + j, 0)),
        ],
        out_shape=[
            jax.ShapeDtypeStruct((b, l, D_FOURIER), _BF16),
            jax.ShapeDtypeStruct((N_GROUPS, b * l // SSM_CHUNK, CHUNK_LANES), _BF16),
        ],
        scratch_shapes=[pltpu.VMEM((nck * ZS_PITCH, LANES), _F32)],
        compiler_params=pltpu.CompilerParams(vmem_limit_bytes=VMEM_LIMIT),
        name="inproj",
    )(x, mod, g_mix.reshape(1, -1), wf_bf16, wst_bf16)


@functools.lru_cache(maxsize=None)
def _dft_tables_np(n):
    n1 = DFT_N1
    n2 = n // n1
    i1 = np.arange(n1)
    ang1 = 2.0 * np.pi * ((i1[:, None] * i1[None, :]) % n1) / n1
    f1 = np.concatenate([np.cos(ang1), -np.sin(ang1)], axis=0).astype(np.float32)
    i2 = np.arange(n2)
    m = (i2[None, None, :] * (i1[:, None, None] + n1 * i2[None, :, None])) % n
    ang = 2.0 * np.pi * m / n
    gr = np.cos(ang) / math.sqrt(n)
    gi = -np.sin(ang) / math.sqrt(n)
    gsz = max(1, DFT_GROUP_ROWS // n2)
    eye = np.eye(gsz)
    bd = lambda m: np.einsum("qjkn,jl->qjkln", m.reshape(n1 // gsz, gsz, n2, n2),
                             eye).reshape(n1 // gsz, gsz * n2, gsz * n2)
    gr, gi = bd(gr), bd(gi)
    g = np.concatenate([np.concatenate([gr, -gi], axis=2),
                        np.concatenate([gi, gr], axis=2)], axis=1).astype(np.float32)
    return f1, g


def _dft1_kernel(f_ref, x_ref, a_ref):
    xt = jnp.swapaxes(x_ref[...], 0, 1)
    for q in range(DFT1_COLS):
        a_ref[:, q * D_FOURIER:(q + 1) * D_FOURIER] = \
            jnp.dot(f_ref[...], xt[q],
                    preferred_element_type=_F32).astype(_BF16)


def _dft_stage1(zf, f1):
    b, l, _ = zf.shape
    n1 = DFT_N1
    n2 = l // n1
    cw = DFT1_COLS * D_FOURIER
    return pl.pallas_call(
        _dft1_kernel,
        grid=(b, n2 // DFT1_COLS),
        in_specs=[
            _const_spec((2 * n1, n1)),
            pl.BlockSpec((None, n1, DFT1_COLS, D_FOURIER), lambda i, j: (i, 0, j, 0)),
        ],
        out_specs=pl.BlockSpec((None, 2 * n1, cw), lambda i, j: (i, 0, j)),
        out_shape=jax.ShapeDtypeStruct((b, 2 * n1, n2 * D_FOURIER), _BF16),
        name="dft_stage1",
    )(f1, zf.reshape(b, n1, n2, D_FOURIER))


def _dft2_kernel(a_ref, g_ref, wc_ref, bf_ref, o_ref, *, kb, n2):
    gsz = kb // g_ref.shape[0]
    rows = gsz * n2
    xr, xi = [], []
    for q in range(kb // gsz):
        ks = slice(q * gsz, (q + 1) * gsz)
        aa = jnp.concatenate([a_ref[0, ks].reshape(rows, D_FOURIER),
                              a_ref[1, ks].reshape(rows, D_FOURIER)], axis=0)
        x = jnp.dot(g_ref[q], aa, preferred_element_type=_F32)
        xr.append(x[:rows].astype(_BF16))
        xi.append(x[rows:].astype(_BF16))
    xr = jnp.concatenate(xr, axis=0)
    xi = jnp.concatenate(xi, axis=0)
    y = jnp.dot(xr, wc_ref[:D_FOURIER, :], preferred_element_type=_F32)
    y = y + jnp.dot(xi, wc_ref[D_FOURIER:, :], preferred_element_type=_F32)
    y = y + bf_ref[...]
    o_ref[...] = y.reshape(kb, n2, D_FOURIER).astype(o_ref.dtype)


def _dft_stage2(a, g, wcomb, b_fourier):
    b = a.shape[0]
    n1 = DFT_N1
    n2 = a.shape[2] // D_FOURIER
    gsz = n1 // g.shape[0]
    kb = max(8, 1024 // n2)
    a5 = a.reshape(b, 2, n1, n2, D_FOURIER)
    return pl.pallas_call(
        functools.partial(_dft2_kernel, kb=kb, n2=n2),
        grid=(b, n1 // kb),
        in_specs=[
            pl.BlockSpec((None, 2, kb, n2, D_FOURIER), lambda i, j: (i, 0, j, 0, 0)),
            pl.BlockSpec((kb // gsz, 2 * gsz * n2, 2 * gsz * n2), lambda i, j: (j, 0, 0)),
            _const_spec((2 * D_FOURIER, D_FOURIER)),
            _const_spec((1, D_FOURIER)),
        ],
        out_specs=pl.BlockSpec((None, kb, n2, D_FOURIER), lambda i, j: (i, j, 0, 0)),
        out_shape=jax.ShapeDtypeStruct((b, n1, n2, D_FOURIER), _BF16),
        name="dft_stage2",
    )(a5, g, wcomb, b_fourier.reshape(1, -1))


def _fourier_mixer(zf, wcomb, b_fourier):
    b, l, _ = zf.shape
    f1_np, g_np = _dft_tables_np(l)
    f1 = jnp.asarray(f1_np).astype(_BF16)
    g = jnp.asarray(g_np).astype(_BF16)
    a = _dft_stage1(zf, f1)
    y = _dft_stage2(a, g, wcomb, b_fourier)
    return jnp.swapaxes(y, 1, 2).reshape(b, l, D_FOURIER)


def _cpow(k, zr, zi):
    mag = jnp.exp(k * zr)
    return mag * jnp.cos(k * zi), mag * jnp.sin(k * zi)


_GELU_C1 = -2.0 * math.sqrt(2.0 / math.pi) * math.log2(math.e)
_GELU_C2 = 0.044715 * _GELU_C1


def _gelu_tanh(x):
    return x / (1.0 + jnp.exp2(x * (_GELU_C1 + _GELU_C2 * (x * x))))


def _dot3(a, b):
    a_hi = a.astype(_BF16)
    b_hi = b.astype(_BF16)
    a_lo = (a - a_hi.astype(_F32)).astype(_BF16)
    b_lo = (b - b_hi.astype(_F32)).astype(_BF16)
    dot = functools.partial(jnp.dot, preferred_element_type=_F32)
    return dot(a_hi, b_hi) + (dot(a_hi, b_lo) + dot(a_lo, b_hi))


def _discretise(a_re, a_im, log_dt):
    lam_re = jnp.minimum(a_re, A_RE_MAX)
    dt = jnp.exp(log_dt)
    return lam_re, a_im, lam_re * dt, a_im * dt


def _ssm_kernel(arow_ref, acol_ref, bt_ref, ct_ref, d_ref, up_ref, us_ref,
                op_ref, os_ref, w_scr, bm_scr, e_scr, ub_scr, hc_scr, *, chunks_p, chunks_s):
    t = SSM_CHUNK
    p = SSM_STATE
    lane_k = lax.broadcasted_iota(jnp.int32, (p, t), 1).astype(_F32)
    sub_k = lax.broadcasted_iota(jnp.int32, (t, p), 0).astype(_F32)
    lane0 = lax.broadcasted_iota(jnp.int32, (p, t), 1) == 0

    zrow, bb = [], []
    for d in range(2):
        lam_re, lam_im, zr, zi = _discretise(arow_ref[d, 0:1, :], arow_ref[d, 1:2, :],
                                             arow_ref[d, 2:3, :])
        th = jnp.tanh(0.5 * zr)
        em1 = 2.0 * th / (1.0 - th)
        e_re = em1 * jnp.cos(zi) - 2.0 * jnp.square(jnp.sin(0.5 * zi))
        e_im = (em1 + 1.0) * jnp.sin(zi)
        den = lam_re * lam_re + lam_im * lam_im
        q_re = (e_re * lam_re + e_im * lam_im) / den
        q_im = (e_im * lam_re - e_re * lam_im) / den
        b_re = bt_ref[d, 0]
        b_im = bt_ref[d, 1]
        bb.append((q_re * b_re - q_im * b_im, q_re * b_im + q_im * b_re))
        zrow.append((zr, zi))

    def c_times(d, pw_re, pw_im):
        re, im = [], []
        for h in range(SSM_GROUP):
            c_re = jnp.broadcast_to(ct_ref[d, 0][:, h:h + 1], (p, t))
            c_im = jnp.broadcast_to(ct_ref[d, 1][:, h:h + 1], (p, t))
            re.append(c_re * pw_re - c_im * pw_im)
            im.append(c_re * pw_im + c_im * pw_re)
        return jnp.concatenate(re, axis=1), jnp.concatenate(im, axis=1)

    _, _, zr0, zi0 = _discretise(acol_ref[0, 0], acol_ref[0, 1], acol_ref[0, 2])
    _, _, zr1, zi1 = _discretise(acol_ref[1, 0], acol_ref[1, 1], acol_ref[1, 2])
    caf_re, caf_im = c_times(0, *_cpow(lane_k, zr0, zi0))
    cab_re, cab_im = c_times(1, *_cpow(float(t) - lane_k, zr1, zi1))
    one = jnp.where(lane0, 1.0, 0.0)
    cb0_re, cb0_im = c_times(1, one, jnp.zeros_like(one))

    e_scr[0 * p:1 * p, :] = caf_re.astype(_BF16)
    e_scr[1 * p:2 * p, :] = (-caf_im).astype(_BF16)
    e_scr[2 * p:3 * p, :] = cab_re.astype(_BF16)
    e_scr[3 * p:4 * p, :] = (-cab_im).astype(_BF16)

    (bbf_re, bbf_im), (bbb_re, bbb_im) = bb
    lhs_f = jnp.concatenate([bbf_re, -bbf_im, bbb_re, -bbb_im], axis=1)
    rhs_f = jnp.concatenate([caf_re, caf_im, cb0_re, cb0_im], axis=0)
    kf = _dot3(lhs_f, rhs_f)
    lhs_b = jnp.concatenate([bbb_re, -bbb_im], axis=1)
    rhs_b = jnp.concatenate([cab_re, cab_im], axis=0)
    kb = _dot3(lhs_b, rhs_b)

    pf_re, pf_im = _cpow(float(t - 1) - sub_k, *zrow[0])
    pb_re, pb_im = _cpow(sub_k, *zrow[1])
    for hp in range(SSM_GROUP):
        f_re, f_im = bbf_re[hp:hp + 1, :], bbf_im[hp:hp + 1, :]
        g_re, g_im = bbb_re[hp:hp + 1, :], bbb_im[hp:hp + 1, :]
        blk = jnp.concatenate([pf_re * f_re - pf_im * f_im, pf_re * f_im + pf_im * f_re,
                               pb_re * g_re - pb_im * g_im, pb_re * g_im + pb_im * g_re],
                              axis=1)
        bm_scr[hp * t:(hp + 1) * t, :] = blk.astype(_BF16)

    sl = STATE_LANES
    rows_p = up_ref.shape[0]
    ub_scr[:rows_p, :] = up_ref[...]
    ub_scr[rows_p:, :] = us_ref[...]
    s_all = jnp.dot(ub_scr[...], bm_scr[...], preferred_element_type=_F32)

    def cmul(x, ar, ai):
        a1 = jnp.concatenate([ar, ar], axis=1)
        a2 = jnp.concatenate([-ai, ai], axis=1)
        return x * a1 + pltpu.roll(x, SSM_STATE, 1) * a2

    n_levels = max(chunks_p, chunks_s).bit_length() - 1
    chunk_pow = []
    for d in range(2):
        ar, ai = _cpow(float(t), *zrow[d])
        levels = [(ar, ai)]
        for _ in range(n_levels - 1):
            ar, ai = ar * ar - ai * ai, 2.0 * ar * ai
            levels.append((ar, ai))
        chunk_pow.append(levels)

    def carry_scan(s, n_chunks, r0):
        rows = s.shape[0]
        pos = lax.broadcasted_iota(jnp.int32, (rows, sl), 0) & (n_chunks - 1)

        def shift_rows(x, n, down):
            if down:
                return pltpu.roll(x, n, 0) * (pos >= n).astype(_F32)
            return pltpu.roll(x, rows - n, 0) * (pos < n_chunks - n).astype(_F32)

        xf = s[:, :sl]
        xb = s[:, sl:]
        dist, level = 1, 0
        while dist < n_chunks:
            xf = xf + cmul(shift_rows(xf, dist, True), *chunk_pow[0][level])
            xb = xb + cmul(shift_rows(xb, dist, False), *chunk_pow[1][level])
            dist *= 2
            level += 1
        hf = cmul(shift_rows(xf, 1, True), *_cpow(1.0, *zrow[0]))
        hc_scr[r0:r0 + rows, :sl] = hf.astype(_BF16)
        hc_scr[r0:r0 + rows, sl:] = shift_rows(xb, 1, False).astype(_BF16)

    carry_scan(s_all[:rows_p], chunks_p, 0)
    carry_scan(s_all[rows_p:], chunks_s, rows_p)

    hb = SSM_GROUP // W_COL_BLOCKS
    for nb in range(W_COL_BLOCKS):
        slot = nb % 2
        for hp in range(SSM_GROUP):
            for hl in range(hb):
                h = nb * hb + hl
                lag = jnp.concatenate([kf[hp:hp + 1, h * t:(h + 1) * t],
                                       kb[hp:hp + 1, h * t:(h + 1) * t]], axis=1)
                skew = pltpu.roll(jnp.broadcast_to(lag, (t, 2 * t)), 0, 1,
                                  stride=1, stride_axis=0)
                w_scr[slot, hp * t:(hp + 1) * t, hl * t:(hl + 1) * t] = \
                    skew[:, :t].astype(_BF16)
        c0, c1 = nb * hb * t, (nb + 1) * hb * t
        y = jnp.dot(ub_scr[...], w_scr[slot], preferred_element_type=_F32)
        y = y + jnp.dot(hc_scr[...], e_scr[:, c0:c1], preferred_element_type=_F32)
        yp = y[:rows_p] + d_ref[:, c0:c1] * up_ref[:, c0:c1].astype(_F32)
        ys = y[rows_p:] + d_ref[:, c0:c1] * us_ref[:, c0:c1].astype(_F32)
        op_ref[:, c0:c1] = _gelu_tanh(yp)
        os_ref[:, c0:c1] = _gelu_tanh(ys)


def _ssm_mixer(up, us, chunks_p, chunks_s, a_re, a_im, log_dt, b_re, b_im, c_re, c_im, d_skip):
    g, p, h, t = N_GROUPS, SSM_STATE, SSM_GROUP, SSM_CHUNK
    ldt = jnp.broadcast_to(log_dt[:, :, None], (2, g, p))
    arow = jnp.stack([a_re, a_im, ldt] + [jnp.zeros_like(a_re)] * 5, axis=2)
    arow = jnp.transpose(arow, (1, 0, 2, 3)).astype(_F32)
    acol = jnp.stack([a_re, a_im, ldt], axis=2)
    acol = jnp.transpose(acol, (1, 0, 2, 3)).astype(_F32)
    acol = jnp.broadcast_to(acol[..., None], (g, 2, 3, p, t))
    bt = jnp.transpose(jnp.stack([b_re, b_im], axis=2), (1, 0, 2, 4, 3)).astype(_F32)
    ct = jnp.transpose(jnp.stack([c_re, c_im], axis=2), (1, 0, 2, 4, 3)).astype(_F32)
    dfl = jnp.repeat(d_skip.astype(_F32).reshape(g, 1, h), t, axis=2)

    rows_p, rows_s = up.shape[1], us.shape[1]
    grp = lambda *tail: pl.BlockSpec((None,) + tail, lambda i: (i,) + (0,) * len(tail))
    return pl.pallas_call(
        functools.partial(_ssm_kernel, chunks_p=chunks_p, chunks_s=chunks_s),
        grid=(g,),
        in_specs=[grp(2, 8, p), grp(2, 3, p, t), grp(2, 2, h, p), grp(2, 2, p, h),
                  grp(1, CHUNK_LANES), grp(rows_p, CHUNK_LANES), grp(rows_s, CHUNK_LANES)],
        out_specs=[grp(rows_p, CHUNK_LANES), grp(rows_s, CHUNK_LANES)],
        out_shape=[jax.ShapeDtypeStruct((g, rows_p, CHUNK_LANES), _F32),
                   jax.ShapeDtypeStruct((g, rows_s, CHUNK_LANES), _F32)],
        scratch_shapes=[
            pltpu.VMEM((2, CHUNK_LANES, CHUNK_LANES // W_COL_BLOCKS), _BF16),
            pltpu.VMEM((CHUNK_LANES, 2 * STATE_LANES), _BF16),
            pltpu.VMEM((2 * STATE_LANES, CHUNK_LANES), _BF16),
            pltpu.VMEM((rows_p + rows_s, CHUNK_LANES), _BF16),
            pltpu.VMEM((rows_p + rows_s, 2 * STATE_LANES), _BF16),
        ],
        compiler_params=pltpu.CompilerParams(vmem_limit_bytes=VMEM_LIMIT),
        name="ssm_chunked",
    )(arow, acol, bt, ct, dfl, up, us)


def _tail_kernel(x_ref, yf_ref, gy_ref, mod_ref, wglu_ref, bglu_ref, gf_ref, gss_ref,
                 wout_ref, gmlp_ref, w1_ref, b1_ref, w2_ref, b2_ref, gfin_ref, o_ref, gs_scr):
    sub = pl.program_id(1) % TAIL_GROUP
    nck = gy_ref.shape[1]

    @pl.when(sub == 0)
    def _():
        def per_group(g, carry):
            for hh in range(SSM_GROUP):
                gs_scr[pl.ds(g * SSM_GROUP + hh, nck, stride=ZS_PITCH), :] = \
                    gy_ref[g, :, hh * SSM_CHUNK:(hh + 1) * SSM_CHUNK]
            return carry
        lax.fori_loop(0, N_GROUPS, per_group, 0)

    per_tile = nck // TAIL_GROUP
    slabs = []
    for c in range(per_tile):
        r0 = pl.multiple_of((sub * per_tile + c) * ZS_PITCH, SUBLANES)
        slabs.append(gs_scr[pl.ds(r0, D_SSM), :].T)
    gs = jnp.concatenate(slabs, axis=0).astype(_BF16)

    gate1 = mod_ref[2:3, :]
    shift2 = mod_ref[3:4, :]
    scale2 = mod_ref[4:5, :]
    gate2 = mod_ref[5:6, :]

    ab = jnp.dot(gs, wglu_ref[...], preferred_element_type=_F32) + bglu_ref[...]
    ys = ab[:, :D_SSM] * jax.nn.sigmoid(ab[:, D_SSM:])
    ysn = _rms(ys) * gss_ref[...]
    yfn = _rms(yf_ref[...].astype(_F32)) * gf_ref[...]
    merged = jnp.concatenate([yfn, ysn], axis=-1).astype(_BF16)
    x1 = x_ref[...] + gate1 * jnp.dot(merged, wout_ref[...], preferred_element_type=_F32)

    h2 = (_rms(x1) * gmlp_ref[...] * (1.0 + scale2) + shift2).astype(_BF16)
    ff = jnp.zeros(x1.shape, _F32)
    for c0 in range(0, D_FF, FF_CHUNK):
        a = jnp.dot(h2, w1_ref[:, c0:c0 + FF_CHUNK], preferred_element_type=_F32)
        a = jnp.maximum(a + b1_ref[:, c0:c0 + FF_CHUNK], 0.0)
        a = (a * a).astype(_BF16)
        ff = ff + jnp.dot(a, w2_ref[c0:c0 + FF_CHUNK, :], preferred_element_type=_F32)
    x2 = x1 + gate2 * (ff + b2_ref[...])
    o_ref[...] = _rms(x2) * gfin_ref[...]


def _tail(x, yf, gy, mod, wglu, b_glu, g_f, g_s, wout, g_mlp, w1, b1, w2, b2, g_final):
    b, l, _ = x.shape
    tm = TOKEN_TILE
    nck = TAIL_GROUP * tm // SSM_CHUNK
    blocks = l // (TAIL_GROUP * tm)
    tok = lambda width: pl.BlockSpec((None, tm, width), lambda i, j: (i, j, 0))
    row = lambda a: a.reshape(1, -1)
    single = lambda shape: pl.BlockSpec(shape, lambda *_: (0,) * len(shape),
                                        pipeline_mode=pl.Buffered(1))
    return pl.pallas_call(
        _tail_kernel,
        grid=(b, l // tm),
        in_specs=[
            tok(D_MODEL), tok(D_FOURIER),
            pl.BlockSpec((N_GROUPS, nck, CHUNK_LANES),
                         lambda i, j: (0, i * blocks + j // TAIL_GROUP, 0)),
            pl.BlockSpec((None, N_MOD, D_MODEL), lambda i, j: (i, 0, 0)),
            single((D_SSM, 2 * D_SSM)), single((1, 2 * D_SSM)),
            single((1, D_FOURIER)), single((1, D_SSM)),
            single((D_MODEL, D_MODEL)), single((1, D_MODEL)),
            single((D_MODEL, D_FF)), single((1, D_FF)),
            single((D_FF, D_MODEL)), single((1, D_MODEL)), single((1, D_MODEL)),
        ],
        out_specs=tok(D_MODEL),
        out_shape=jax.ShapeDtypeStruct((b, l, D_MODEL), _F32),
        scratch_shapes=[pltpu.VMEM((nck * ZS_PITCH, LANES), _F32)],
        compiler_params=pltpu.CompilerParams(
            dimension_semantics=("arbitrary", "arbitrary"), vmem_limit_bytes=VMEM_LIMIT),
        name="tail",
    )(x, yf, gy, mod, wglu, row(b_glu), row(g_f), row(g_s), wout, row(g_mlp),
      w1, row(b1), w2, row(b2), row(g_final))


def kernel(x_prompt, x_sample, c_prompt, c_sample, w_ada, b_ada, g_mix_norm, w_in, w_fourier, b_fourier, ssm_a_re, ssm_a_im, ssm_log_dt, ssm_b_re, ssm_b_im, ssm_c_re, ssm_c_im, ssm_d, w_glu, b_glu, g_fourier_out, g_ssm_out, w_out, g_mlp_norm, w_mlp_in, b_mlp_in, w_mlp_out, b_mlp_out, g_final):
    assert w_ada.shape[0] == 1, "single-layer block"
    n_p = c_prompt.shape[0]
    n_s = c_sample.shape[0]
    pad = (-(n_p + n_s)) % 8
    c_all = jnp.concatenate(
        [c_prompt, c_sample, jnp.zeros((pad, D_MODEL), c_prompt.dtype)], axis=0)
    mod = _adaln(c_all, w_ada[0], b_ada[0]).reshape(-1, N_MOD, D_MODEL)

    wf_b = w_in[0][:, :D_FOURIER].astype(_BF16)
    wst_b = w_in[0][:, D_FOURIER:].T.astype(_BF16)
    wcomb = _fold_fourier_weight(w_fourier[0])
    wglu_b = w_glu[0].astype(_BF16)
    wout_b = w_out[0].astype(_BF16)
    w1_b = w_mlp_in[0].astype(_BF16)
    w2_b = w_mlp_out[0].astype(_BF16)
    mod_p = mod[:n_p]
    mod_s = mod[n_p:n_p + n_s]

    zf_p, u_p = _inproj(x_prompt, mod_p, g_mix_norm[0], wf_b, wst_b)
    zf_s, u_s = _inproj(x_sample, mod_s, g_mix_norm[0], wf_b, wst_b)
    gy_p, gy_s = _ssm_mixer(u_p, u_s, x_prompt.shape[1] // SSM_CHUNK,
                            x_sample.shape[1] // SSM_CHUNK, ssm_a_re[0], ssm_a_im[0],
                            ssm_log_dt[0], ssm_b_re[0], ssm_b_im[0], ssm_c_re[0],
                            ssm_c_im[0], ssm_d[0])

    def finish(x, m, zf, gy):
        yf = _fourier_mixer(zf, wcomb, b_fourier[0])
        return _tail(x, yf, gy, m, wglu_b, b_glu[0], g_fourier_out[0], g_ssm_out[0],
                     wout_b, g_mlp_norm[0], w1_b, b_mlp_in[0], w2_b, b_mlp_out[0], g_final)

    return (finish(x_prompt, mod_p, zf_p, gy_p), finish(x_sample, mod_s, zf_s, gy_s))
```

```python
import functools
import math

import numpy as np
import jax
import jax.numpy as jnp
from jax import lax
from jax.experimental import pallas as pl
from jax.experimental.pallas import tpu as pltpu

D_MODEL = 1024
D_FOURIER = 512
D_SSM = 512
N_HEADS = 4
HEAD_DIM = 128
SSM_GROUP = 16
N_GROUPS = 32
SSM_STATE = 64
D_FF = 4096
N_MOD = 6
EPS = 1e-6
A_RE_MAX = -1e-4

SSM_CHUNK = 128
CHUNK_LANES = SSM_CHUNK * SSM_GROUP
STATE_LANES = 2 * SSM_STATE
W_COL_BLOCKS = 4
DFT_N1 = 128
DFT_GROUP_ROWS = 128
DFT1_COLS = 16
LANES = 128
SUBLANES = 8
ROW_PAD = 8
ZS_PITCH = D_SSM + ROW_PAD
INPROJ_TILE = 2048
INPROJ_SUB = 256
TOKEN_TILE = 1024
TAIL_GROUP = 1
FF_CHUNK = 1024
VMEM_LIMIT = 56 * 1024 * 1024

_BF16 = jnp.bfloat16
_F32 = jnp.float32


def _rms(v):
    return v * lax.rsqrt(jnp.mean(v * v, axis=-1, keepdims=True) + EPS)


def _const_spec(shape):
    nd = len(shape)
    return pl.BlockSpec(shape, lambda *_: (0,) * nd)


def _adaln_kernel(c_ref, w_ref, b_ref, o_ref):
    c = c_ref[...]
    s = c * jax.nn.sigmoid(c)
    o_ref[...] = _dot3(s, w_ref[...]) + b_ref[...]


def _adaln(c_all, w_ada, b_ada):
    rows = c_all.shape[0]
    return pl.pallas_call(
        _adaln_kernel,
        grid=(N_MOD,),
        in_specs=[
            _const_spec((rows, D_MODEL)),
            pl.BlockSpec((D_MODEL, D_MODEL), lambda j: (0, j)),
            pl.BlockSpec((1, D_MODEL), lambda j: (0, j)),
        ],
        out_specs=pl.BlockSpec((rows, D_MODEL), lambda j: (0, j)),
        out_shape=jax.ShapeDtypeStruct((rows, N_MOD * D_MODEL), _F32),
        name="adaln",
    )(c_all, w_ada, b_ada.reshape(1, -1))


def _fold_kernel(cs_ref, wf_ref, o_ref):
    o_ref[...] = _dot3(cs_ref[...], wf_ref[...]).astype(o_ref.dtype)


@functools.lru_cache(maxsize=None)
def _channel_dft_np():
    k = np.arange(D_FOURIER)
    ang = 2.0 * np.pi * ((k[:, None] * k[None, :]) % D_FOURIER) / D_FOURIER
    scale = 1.0 / math.sqrt(D_FOURIER)
    return np.concatenate([np.cos(ang), np.sin(ang)], axis=0).astype(np.float32) * scale


def _fold_fourier_weight(w_fourier):
    wblk = jnp.zeros((D_FOURIER, D_FOURIER), _F32)
    for h in range(N_HEADS):
        sl = slice(h * HEAD_DIM, (h + 1) * HEAD_DIM)
        wblk = wblk.at[sl, sl].set(w_fourier[h])
    cs = jnp.asarray(_channel_dft_np())
    return pl.pallas_call(
        _fold_kernel,
        out_shape=jax.ShapeDtypeStruct((2 * D_FOURIER, D_FOURIER), _BF16),
        name="fold_fourier_weight",
    )(cs, wblk)


def _inproj_kernel(x_ref, mod_ref, g_ref, wf_ref, wst_ref, zf_ref, u_ref, zs_scr):
    tm = x_ref.shape[0]
    nck = tm // SSM_CHUNK
    for s0 in range(0, tm, INPROJ_SUB):
        x = x_ref[s0:s0 + INPROJ_SUB, :]
        h = (_rms(x) * g_ref[...] * (1.0 + mod_ref[1:2, :]) + mod_ref[0:1, :]).astype(_BF16)
        zf_ref[s0:s0 + INPROJ_SUB, :] = jnp.dot(
            h, wf_ref[...], preferred_element_type=_F32).astype(_BF16)
        zst = lax.dot_general(wst_ref[...], h, (((1,), (1,)), ((), ())),
                              preferred_element_type=_F32)
        for c in range(s0 // SSM_CHUNK, (s0 + INPROJ_SUB) // SSM_CHUNK):
            zs_scr[c * ZS_PITCH:c * ZS_PITCH + D_SSM, :] = \
                zst[:, c * SSM_CHUNK - s0:(c + 1) * SSM_CHUNK - s0]

    for g in range(N_GROUPS):
        for hh in range(SSM_GROUP):
            u_ref[g, :, hh * SSM_CHUNK:(hh + 1) * SSM_CHUNK] = \
                zs_scr[pl.ds(g * SSM_GROUP + hh, nck, stride=ZS_PITCH), :].astype(_BF16)


def _inproj(x, mod, g_mix, wf_bf16, wst_bf16):
    b, l, _ = x.shape
    tm = INPROJ_TILE
    nck = tm // SSM_CHUNK
    steps = l // tm
    return pl.pallas_call(
        _inproj_kernel,
        grid=(b, steps),
        in_specs=[
            pl.BlockSpec((None, tm, D_MODEL), lambda i, j: (i, j, 0)),
            pl.BlockSpec((None, N_MOD, D_MODEL), lambda i, j: (i, 0, 0)),
            _const_spec((1, D_MODEL)),
            _const_spec((D_MODEL, D_FOURIER)),
            _const_spec((D_SSM, D_MODEL)),
        ],
        out_specs=[
            pl.BlockSpec((None, tm, D_FOURIER), lambda i, j: (i, j, 0)),
            pl.BlockSpec((N_GROUPS, nck, CHUNK_LANES), lambda i, j: (0, i * steps + j, 0)),
        ],
        out_shape=[
            jax.ShapeDtypeStruct((b, l, D_FOURIER), _BF16),
            jax.ShapeDtypeStruct((N_GROUPS, b * l // SSM_CHUNK, CHUNK_LANES), _BF16),
        ],
        scratch_shapes=[pltpu.VMEM((nck * ZS_PITCH, LANES), _F32)],
        compiler_params=pltpu.CompilerParams(vmem_limit_bytes=VMEM_LIMIT),
        name="inproj",
    )(x, mod, g_mix.reshape(1, -1), wf_bf16, wst_bf16)


@functools.lru_cache(maxsize=None)
def _dft_tables_np(n):
    n1 = DFT_N1
    n2 = n // n1
    i1 = np.arange(n1)
    ang1 = 2.0 * np.pi * ((i1[:, None] * i1[None, :]) % n1) / n1
    f1 = np.concatenate([np.cos(ang1), -np.sin(ang1)], axis=0).astype(np.float32)
    i2 = np.arange(n2)
    m = (i2[None, None, :] * (i1[:, None, None] + n1 * i2[None, :, None])) % n
    ang = 2.0 * np.pi * m / n
    gr = np.cos(ang) / math.sqrt(n)
    gi = -np.sin(ang) / math.sqrt(n)
    gsz = max(1, DFT_GROUP_ROWS // n2)
    eye = np.eye(gsz)
    bd = lambda m: np.einsum("qjkn,jl->qjkln", m.reshape(n1 // gsz, gsz, n2, n2),
                             eye).reshape(n1 // gsz, gsz * n2, gsz * n2)
    gr, gi = bd(gr), bd(gi)
    g = np.concatenate([np.concatenate([gr, -gi], axis=2),
                        np.concatenate([gi, gr], axis=2)], axis=1).astype(np.float32)
    return f1, g


def _dft1_kernel(f_ref, x_ref, a_ref):
    xt = jnp.swapaxes(x_ref[...], 0, 1)
    for q in range(DFT1_COLS):
        a_ref[:, q * D_FOURIER:(q + 1) * D_FOURIER] = \
            jnp.dot(f_ref[...], xt[q],
                    preferred_element_type=_F32).astype(_BF16)


def _dft_stage1(zf, f1):
    b, l, _ = zf.shape
    n1 = DFT_N1
    n2 = l // n1
    cw = DFT1_COLS * D_FOURIER
    return pl.pallas_call(
        _dft1_kernel,
        grid=(b, n2 // DFT1_COLS),
        in_specs=[
            _const_spec((2 * n1, n1)),
            pl.BlockSpec((None, n1, DFT1_COLS, D_FOURIER), lambda i, j: (i, 0, j, 0)),
        ],
        out_specs=pl.BlockSpec((None, 2 * n1, cw), lambda i, j: (i, 0, j)),
        out_shape=jax.ShapeDtypeStruct((b, 2 * n1, n2 * D_FOURIER), _BF16),
        name="dft_stage1",
    )(f1, zf.reshape(b, n1, n2, D_FOURIER))


def _dft2_kernel(a_ref, g_ref, wc_ref, bf_ref, o_ref, *, kb, n2):
    gsz = kb // g_ref.shape[0]
    rows = gsz * n2
    xr, xi = [], []
    for q in range(kb // gsz):
        ks = slice(q * gsz, (q + 1) * gsz)
        aa = jnp.concatenate([a_ref[0, ks].reshape(rows, D_FOURIER),
                              a_ref[1, ks].reshape(rows, D_FOURIER)], axis=0)
        x = jnp.dot(g_ref[q], aa, preferred_element_type=_F32)
        xr.append(x[:rows].astype(_BF16))
        xi.append(x[rows:].astype(_BF16))
    xr = jnp.concatenate(xr, axis=0)
    xi = jnp.concatenate(xi, axis=0)
    y = jnp.dot(xr, wc_ref[:D_FOURIER, :], preferred_element_type=_F32)
    y = y + jnp.dot(xi, wc_ref[D_FOURIER:, :], preferred_element_type=_F32)
    y = y + bf_ref[...]
    o_ref[...] = y.reshape(kb, n2, D_FOURIER).astype(o_ref.dtype)


def _dft_stage2(a, g, wcomb, b_fourier):
    b = a.shape[0]
    n1 = DFT_N1
    n2 = a.shape[2] // D_FOURIER
    gsz = n1 // g.shape[0]
    kb = max(8, 1024 // n2)
    a5 = a.reshape(b, 2, n1, n2, D_FOURIER)
    return pl.pallas_call(
        functools.partial(_dft2_kernel, kb=kb, n2=n2),
        grid=(b, n1 // kb),
        in_specs=[
            pl.BlockSpec((None, 2, kb, n2, D_FOURIER), lambda i, j: (i, 0, j, 0, 0)),
            pl.BlockSpec((kb // gsz, 2 * gsz * n2, 2 * gsz * n2), lambda i, j: (j, 0, 0)),
            _const_spec((2 * D_FOURIER, D_FOURIER)),
            _const_spec((1, D_FOURIER)),
        ],
        out_specs=pl.BlockSpec((None, kb, n2, D_FOURIER), lambda i, j: (i, j, 0, 0)),
        out_shape=jax.ShapeDtypeStruct((b, n1, n2, D_FOURIER), _BF16),
        name="dft_stage2",
    )(a5, g, wcomb, b_fourier.reshape(1, -1))


def _fourier_mixer(zf, wcomb, b_fourier):
    b, l, _ = zf.shape
    f1_np, g_np = _dft_tables_np(l)
    f1 = jnp.asarray(f1_np).astype(_BF16)
    g = jnp.asarray(g_np).astype(_BF16)
    a = _dft_stage1(zf, f1)
    y = _dft_stage2(a, g, wcomb, b_fourier)
    return jnp.swapaxes(y, 1, 2).reshape(b, l, D_FOURIER)


def _cpow(k, zr, zi):
    mag = jnp.exp(k * zr)
    return mag * jnp.cos(k * zi), mag * jnp.sin(k * zi)


_GELU_C1 = -2.0 * math.sqrt(2.0 / math.pi) * math.log2(math.e)
_GELU_C2 = 0.044715 * _GELU_C1


def _gelu_tanh(x):
    return x / (1.0 + jnp.exp2(x * (_GELU_C1 + _GELU_C2 * (x * x))))


def _dot3(a, b):
    a_hi = a.astype(_BF16)
    b_hi = b.astype(_BF16)
    a_lo = (a - a_hi.astype(_F32)).astype(_BF16)
    b_lo = (b - b_hi.astype(_F32)).astype(_BF16)
    dot = functools.partial(jnp.dot, preferred_element_type=_F32)
    return dot(a_hi, b_hi) + (dot(a_hi, b_lo) + dot(a_lo, b_hi))


def _discretise(a_re, a_im, log_dt):
    lam_re = jnp.minimum(a_re, A_RE_MAX)
    dt = jnp.exp(log_dt)
    return lam_re, a_im, lam_re * dt, a_im * dt


def _ssm_kernel(arow_ref, acol_ref, bt_ref, ct_ref, d_ref, up_ref, us_ref,
                op_ref, os_ref, w_scr, bm_scr, e_scr, ub_scr, hc_scr, *, chunks_p, chunks_s):
    t = SSM_CHUNK
    p = SSM_STATE
    lane_k = lax.broadcasted_iota(jnp.int32, (p, t), 1).astype(_F32)
    sub_k = lax.broadcasted_iota(jnp.int32, (t, p), 0).astype(_F32)
    lane0 = lax.broadcasted_iota(jnp.int32, (p, t), 1) == 0

    zrow, bb = [], []
    for d in range(2):
        lam_re, lam_im, zr, zi = _discretise(arow_ref[d, 0:1, :], arow_ref[d, 1:2, :],
                                             arow_ref[d, 2:3, :])
        th = jnp.tanh(0.5 * zr)
        em1 = 2.0 * th / (1.0 - th)
        e_re = em1 * jnp.cos(zi) - 2.0 * jnp.square(jnp.sin(0.5 * zi))
        e_im = (em1 + 1.0) * jnp.sin(zi)
        den = lam_re * lam_re + lam_im * lam_im
        q_re = (e_re * lam_re + e_im * lam_im) / den
        q_im = (e_im * lam_re - e_re * lam_im) / den
        b_re = bt_ref[d, 0]
        b_im = bt_ref[d, 1]
        bb.append((q_re * b_re - q_im * b_im, q_re * b_im + q_im * b_re))
        zrow.append((zr, zi))

    def c_times(d, pw_re, pw_im):
        re, im = [], []
        for h in range(SSM_GROUP):
            c_re = jnp.broadcast_to(ct_ref[d, 0][:, h:h + 1], (p, t))
            c_im = jnp.broadcast_to(ct_ref[d, 1][:, h:h + 1], (p, t))
            re.append(c_re * pw_re - c_im * pw_im)
            im.append(c_re * pw_im + c_im * pw_re)
        return jnp.concatenate(re, axis=1), jnp.concatenate(im, axis=1)

    _, _, zr0, zi0 = _discretise(acol_ref[0, 0], acol_ref[0, 1], acol_ref[0, 2])
    _, _, zr1, zi1 = _discretise(acol_ref[1, 0], acol_ref[1, 1], acol_ref[1, 2])
    caf_re, caf_im = c_times(0, *_cpow(lane_k, zr0, zi0))
    cab_re, cab_im = c_times(1, *_cpow(float(t) - lane_k, zr1, zi1))
    one = jnp.where(lane0, 1.0, 0.0)
    cb0_re, cb0_im = c_times(1, one, jnp.zeros_like(one))

    e_scr[0 * p:1 * p, :] = caf_re.astype(_BF16)
    e_scr[1 * p:2 * p, :] = (-caf_im).astype(_BF16)
    e_scr[2 * p:3 * p, :] = cab_re.astype(_BF16)
    e_scr[3 * p:4 * p, :] = (-cab_im).astype(_BF16)

    (bbf_re, bbf_im), (bbb_re, bbb_im) = bb
    lhs_f = jnp.concatenate([bbf_re, -bbf_im, bbb_re, -bbb_im], axis=1)
    rhs_f = jnp.concatenate([caf_re, caf_im, cb0_re, cb0_im], axis=0)
    kf = _dot3(lhs_f, rhs_f)
    lhs_b = jnp.concatenate([bbb_re, -bbb_im], axis=1)
    rhs_b = jnp.concatenate([cab_re, cab_im], axis=0)
    kb = _dot3(lhs_b, rhs_b)

    pf_re, pf_im = _cpow(float(t - 1) - sub_k, *zrow[0])
    pb_re, pb_im = _cpow(sub_k, *zrow[1])
    for hp in range(SSM_GROUP):
        f_re, f_im = bbf_re[hp:hp + 1, :], bbf_im[hp:hp + 1, :]
        g_re, g_im = bbb_re[hp:hp + 1, :], bbb_im[hp:hp + 1, :]
        blk = jnp.concatenate([pf_re * f_re - pf_im * f_im, pf_re * f_im + pf_im * f_re,
                               pb_re * g_re - pb_im * g_im, pb_re * g_im + pb_im * g_re],
                              axis=1)
        bm_scr[hp * t:(hp + 1) * t, :] = blk.astype(_BF16)

    sl = STATE_LANES
    rows_p = up_ref.shape[0]
    ub_scr[:rows_p, :] = up_ref[...]
    ub_scr[rows_p:, :] = us_ref[...]
    s_all = jnp.dot(ub_scr[...], bm_scr[...], preferred_element_type=_F32)

    def cmul(x, ar, ai):
        a1 = jnp.concatenate([ar, ar], axis=1)
        a2 = jnp.concatenate([-ai, ai], axis=1)
        return x * a1 + pltpu.roll(x, SSM_STATE, 1) * a2

    n_levels = max(chunks_p, chunks_s).bit_length() - 1
    chunk_pow = []
    for d in range(2):
        ar, ai = _cpow(float(t), *zrow[d])
        levels = [(ar, ai)]
        for _ in range(n_levels - 1):
            ar, ai = ar * ar - ai * ai, 2.0 * ar * ai
            levels.append((ar, ai))
        chunk_pow.append(levels)

    def carry_scan(s, n_chunks, r0):
        rows = s.shape[0]
        pos = lax.broadcasted_iota(jnp.int32, (rows, sl), 0) & (n_chunks - 1)

        def shift_rows(x, n, down):
            if down:
                return pltpu.roll(x, n, 0) * (pos >= n).astype(_F32)
            return pltpu.roll(x, rows - n, 0) * (pos < n_chunks - n).astype(_F32)

        xf = s[:, :sl]
        xb = s[:, sl:]
        dist, level = 1, 0
        while dist < n_chunks:
            xf = xf + cmul(shift_rows(xf, dist, True), *chunk_pow[0][level])
            xb = xb + cmul(shift_rows(xb, dist, False), *chunk_pow[1][level])
            dist *= 2
            level += 1
        hf = cmul(shift_rows(xf, 1, True), *_cpow(1.0, *zrow[0]))
        hc_scr[r0:r0 + rows, :sl] = hf.astype(_BF16)
        hc_scr[r0:r0 + rows, sl:] = shift_rows(xb, 1, False).astype(_BF16)

    carry_scan(s_all[:rows_p], chunks_p, 0)
    carry_scan(s_all[rows_p:], chunks_s, rows_p)

    hb = SSM_GROUP // W_COL_BLOCKS
    for nb in range(W_COL_BLOCKS):
        slot = nb % 2
        for hp in range(SSM_GROUP):
            for hl in range(hb):
                h = nb * hb + hl
                lag = jnp.concatenate([kf[hp:hp + 1, h * t:(h + 1) * t],
                                       kb[hp:hp + 1, h * t:(h + 1) * t]], axis=1)
                skew = pltpu.roll(jnp.broadcast_to(lag, (t, 2 * t)), 0, 1,
                                  stride=1, stride_axis=0)
                w_scr[slot, hp * t:(hp + 1) * t, hl * t:(hl + 1) * t] = \
                    skew[:, :t].astype(_BF16)
        c0, c1 = nb * hb * t, (nb + 1) * hb * t
        y = jnp.dot(ub_scr[...], w_scr[slot], preferred_element_type=_F32)
        y = y + jnp.dot(hc_scr[...], e_scr[:, c0:c1], preferred_element_type=_F32)
        yp = y[:rows_p] + d_ref[:, c0:c1] * up_ref[:, c0:c1].astype(_F32)
        ys = y[rows_p:] + d_ref[:, c0:c1] * us_ref[:, c0:c1].astype(_F32)
        op_ref[:, c0:c1] = _gelu_tanh(yp)
        os_ref[:, c0:c1] = _gelu_tanh(ys)


def _ssm_mixer(up, us, chunks_p, chunks_s, a_re, a_im, log_dt, b_re, b_im, c_re, c_im, d_skip):
    g, p, h, t = N_GROUPS, SSM_STATE, SSM_GROUP, SSM_CHUNK
    ldt = jnp.broadcast_to(log_dt[:, :, None], (2, g, p))
    arow = jnp.stack([a_re, a_im, ldt] + [jnp.zeros_like(a_re)] * 5, axis=2)
    arow = jnp.transpose(arow, (1, 0, 2, 3)).astype(_F32)
    acol = jnp.stack([a_re, a_im, ldt], axis=2)
    acol = jnp.transpose(acol, (1, 0, 2, 3)).astype(_F32)
    acol = jnp.broadcast_to(acol[..., None], (g, 2, 3, p, t))
    bt = jnp.transpose(jnp.stack([b_re, b_im], axis=2), (1, 0, 2, 4, 3)).astype(_F32)
    ct = jnp.transpose(jnp.stack([c_re, c_im], axis=2), (1, 0, 2, 4, 3)).astype(_F32)
    dfl = jnp.repeat(d_skip.astype(_F32).reshape(g, 1, h), t, axis=2)

    rows_p, rows_s = up.shape[1], us.shape[1]
    grp = lambda *tail: pl.BlockSpec((None,) + tail, lambda i: (i,) + (0,) * len(tail))
    return pl.pallas_call(
        functools.partial(_ssm_kernel, chunks_p=chunks_p, chunks_s=chunks_s),
        grid=(g,),
        in_specs=[grp(2, 8, p), grp(2, 3, p, t), grp(2, 2, h, p), grp(2, 2, p, h),
                  grp(1, CHUNK_LANES), grp(rows_p, CHUNK_LANES), grp(rows_s, CHUNK_LANES)],
        out_specs=[grp(rows_p, CHUNK_LANES), grp(rows_s, CHUNK_LANES)],
        out_shape=[jax.ShapeDtypeStruct((g, rows_p, CHUNK_LANES), _F32),
                   jax.ShapeDtypeStruct((g, rows_s, CHUNK_LANES), _F32)],
        scratch_shapes=[
            pltpu.VMEM((2, CHUNK_LANES, CHUNK_LANES // W_COL_BLOCKS), _BF16),
            pltpu.VMEM((CHUNK_LANES, 2 * STATE_LANES), _BF16),
            pltpu.VMEM((2 * STATE_LANES, CHUNK_LANES), _BF16),
            pltpu.VMEM((rows_p + rows_s, CHUNK_LANES), _BF16),
            pltpu.VMEM((rows_p + rows_s, 2 * STATE_LANES), _BF16),
        ],
        compiler_params=pltpu.CompilerParams(vmem_limit_bytes=VMEM_LIMIT),
        name="ssm_chunked",
    )(arow, acol, bt, ct, dfl, up, us)


def _tail_kernel(x_ref, yf_ref, gy_ref, mod_ref, wglu_ref, bglu_ref, gf_ref, gss_ref,
                 wout_ref, gmlp_ref, w1_ref, b1_ref, w2_ref, b2_ref, gfin_ref, o_ref, gs_scr):
    sub = pl.program_id(1) % TAIL_GROUP
    nck = gy_ref.shape[1]

    @pl.when(sub == 0)
    def _():
        def per_group(g, carry):
            for hh in range(SSM_GROUP):
                gs_scr[pl.ds(g * SSM_GROUP + hh, nck, stride=ZS_PITCH), :] = \
                    gy_ref[g, :, hh * SSM_CHUNK:(hh + 1) * SSM_CHUNK]
            return carry
        lax.fori_loop(0, N_GROUPS, per_group, 0)

    per_tile = nck // TAIL_GROUP
    slabs = []
    for c in range(per_tile):
        r0 = pl.multiple_of((sub * per_tile + c) * ZS_PITCH, SUBLANES)
        slabs.append(gs_scr[pl.ds(r0, D_SSM), :].T)
    gs = jnp.concatenate(slabs, axis=0).astype(_BF16)

    gate1 = mod_ref[2:3, :]
    shift2 = mod_ref[3:4, :]
    scale2 = mod_ref[4:5, :]
    gate2 = mod_ref[5:6, :]

    ab = jnp.dot(gs, wglu_ref[...], preferred_element_type=_F32) + bglu_ref[...]
    ys = ab[:, :D_SSM] * jax.nn.sigmoid(ab[:, D_SSM:])
    ysn = _rms(ys) * gss_ref[...]
    yfn = _rms(yf_ref[...].astype(_F32)) * gf_ref[...]
    merged = jnp.concatenate([yfn, ysn], axis=-1).astype(_BF16)
    x1 = x_ref[...] + gate1 * jnp.dot(merged, wout_ref[...], preferred_element_type=_F32)

    h2 = (_rms(x1) * gmlp_ref[...] * (1.0 + scale2) + shift2).astype(_BF16)
    ff = jnp.zeros(x1.shape, _F32)
    for c0 in range(0, D_FF, FF_CHUNK):
        a = jnp.dot(h2, w1_ref[:, c0:c0 + FF_CHUNK], preferred_element_type=_F32)
        a = jnp.maximum(a + b1_ref[:, c0:c0 + FF_CHUNK], 0.0)
        a = (a * a).astype(_BF16)
        ff = ff + jnp.dot(a, w2_ref[c0:c0 + FF_CHUNK, :], preferred_element_type=_F32)
    x2 = x1 + gate2 * (ff + b2_ref[...])
    o_ref[...] = _rms(x2) * gfin_ref[...]


def _tail(x, yf, gy, mod, wglu, b_glu, g_f, g_s, wout, g_mlp, w1, b1, w2, b2, g_final):
    b, l, _ = x.shape
    tm = TOKEN_TILE
    nck = TAIL_GROUP * tm // SSM_CHUNK
    blocks = l // (TAIL_GROUP * tm)
    tok = lambda width: pl.BlockSpec((None, tm, width), lambda i, j: (i, j, 0))
    row = lambda a: a.reshape(1, -1)
    single = lambda shape: pl.BlockSpec(shape, lambda *_: (0,) * len(shape),
                                        pipeline_mode=pl.Buffered(1))
    return pl.pallas_call(
        _tail_kernel,
        grid=(b, l // tm),
        in_specs=[
            tok(D_MODEL), tok(D_FOURIER),
            pl.BlockSpec((N_GROUPS, nck, CHUNK_LANES),
                         lambda i, j: (0, i * blocks + j // TAIL_GROUP, 0)),
            pl.BlockSpec((None, N_MOD, D_MODEL), lambda i, j: (i, 0, 0)),
            single((D_SSM, 2 * D_SSM)), single((1, 2 * D_SSM)),
            single((1, D_FOURIER)), single((1, D_SSM)),
            single((D_MODEL, D_MODEL)), single((1, D_MODEL)),
            single((D_MODEL, D_FF)), single((1, D_FF)),
            single((D_FF, D_MODEL)), single((1, D_MODEL)), single((1, D_MODEL)),
        ],
        out_specs=tok(D_MODEL),
        out_shape=jax.ShapeDtypeStruct((b, l, D_MODEL), _F32),
        scratch_shapes=[pltpu.VMEM((nck * ZS_PITCH, LANES), _F32)],
        compiler_params=pltpu.CompilerParams(
            dimension_semantics=("arbitrary", "arbitrary"), vmem_limit_bytes=VMEM_LIMIT),
        name="tail",
    )(x, yf, gy, mod, wglu, row(b_glu), row(g_f), row(g_s), wout, row(g_mlp),
      w1, row(b1), w2, row(b2), row(g_final))


def kernel(x_prompt, x_sample, c_prompt, c_sample, w_ada, b_ada, g_mix_norm, w_in, w_fourier, b_fourier, ssm_a_re, ssm_a_im, ssm_log_dt, ssm_b_re, ssm_b_im, ssm_c_re, ssm_c_im, ssm_d, w_glu, b_glu, g_fourier_out, g_ssm_out, w_out, g_mlp_norm, w_mlp_in, b_mlp_in, w_mlp_out, b_mlp_out, g_final):
    assert w_ada.shape[0] == 1, "single-layer block"
    n_p = c_prompt.shape[0]
    n_s = c_sample.shape[0]
    pad = (-(n_p + n_s)) % 8
    c_all = jnp.concatenate(
        [c_prompt, c_sample, jnp.zeros((pad, D_MODEL), c_prompt.dtype)], axis=0)
    mod = _adaln(c_all, w_ada[0], b_ada[0]).reshape(-1, N_MOD, D_MODEL)

    wf_b = w_in[0][:, :D_FOURIER].astype(_BF16)
    wst_b = w_in[0][:, D_FOURIER:].T.astype(_BF16)
    wcomb = _fold_fourier_weight(w_fourier[0])
    wglu_b = w_glu[0].astype(_BF16)
    wout_b = w_out[0].astype(_BF16)
    w1_b = w_mlp_in[0].astype(_BF16)
    w2_b = w_mlp_out[0].astype(_BF16)
    mod_p = mod[:n_p]
    mod_s = mod[n_p:n_p + n_s]

    zf_p, u_p = _inproj(x_prompt, mod_p, g_mix_norm[0], wf_b, wst_b)
    zf_s, u_s = _inproj(x_sample, mod_s, g_mix_norm[0], wf_b, wst_b)
    gy_p, gy_s = _ssm_mixer(u_p, u_s, x_prompt.shape[1] // SSM_CHUNK,
                            x_sample.shape[1] // SSM_CHUNK, ssm_a_re[0], ssm_a_im[0],
                            ssm_log_dt[0], ssm_b_re[0], ssm_b_im[0], ssm_c_re[0],
                            ssm_c_im[0], ssm_d[0])

    def finish(x, m, zf, gy):
        yf = _fourier_mixer(zf, wcomb, b_fourier[0])
        return _tail(x, yf, gy, m, wglu_b, b_glu[0], g_fourier_out[0], g_ssm_out[0],
                     wout_b, g_mlp_norm[0], w1_b, b_mlp_in[0], w2_b, b_mlp_out[0], g_final)

    return (finish(x_prompt, mod_p, zf_p, gy_p), finish(x_sample, mod_s, zf_s, gy_s))
```
